```python
import jax, jax.numpy as jnp
from jax import lax
import numpy as np

D_MODEL = 1024
BATCH = 32
SEQ = 2048
DEPTH = 1

CTX_LEN = 256
GRID_W = 64
D_F = D_MODEL // 2
N_FG = 4
FG = D_F // N_FG
D_M = D_MODEL // 2
N_H = 4
DH = D_M // N_H
CHUNK = 64
CONV_K = 3
N_GATE_KINDS = 4
FORGET_BIAS_LO = 3.0
FORGET_BIAS_HI = 6.0
N_BRANCH = 2
N_EXPERTS = 16
CAP_FACTOR = 2
D_FF_E = D_MODEL
EPS = 1e-6
OFF_F = 0
OFF_QK = OFF_F + D_F
OFF_V = OFF_QK + 2 * D_M
OFF_O = OFF_V + D_M
OFF_G = OFF_O + D_M
OFF_BR = OFF_G + N_GATE_KINDS * N_H
IN_COLS = OFF_BR + N_BRANCH * D_MODEL

kernel_name = 'hybrid_fourier_mlstm_ec_block'


def rmsnorm(x, g):
    xf = x.astype(jnp.float32)
    y = xf * lax.rsqrt(jnp.mean(xf * xf, axis=-1, keepdims=True) + EPS)
    return (y * g.astype(jnp.float32)).astype(x.dtype)


def modulate(h, shift, scale):
    return h * (1 + scale) + shift


def grid_conv(u, w, rows):
    b, l, ch = u.shape
    ug = u.reshape(b, rows, l // rows, ch)
    out = lax.conv_general_dilated(ug, w[:, :, None, :].astype(u.dtype), (1, 1), 'SAME',
                                   dimension_numbers=('NHWC', 'HWIO', 'NHWC'),
                                   feature_group_count=ch)
    return out.reshape(b, l, ch)


def mlstm_inputs(p, rows, w_conv, b_gates):
    b, l, _ = p.shape
    qk = jax.nn.silu(grid_conv(p[..., OFF_QK:OFF_V], w_conv, rows))
    to_heads = lambda a: a.reshape(b, l, N_H, DH).transpose(0, 2, 1, 3).astype(jnp.float32)
    q = to_heads(qk[..., :D_M])
    k = to_heads(qk[..., D_M:]) * (DH ** -0.5)
    v = to_heads(p[..., OFF_V:OFF_O])
    g = (p[..., OFF_G:OFF_BR].astype(jnp.float32).reshape(b, l, N_GATE_KINDS, N_H).transpose(2, 0, 3, 1)
         + b_gates.astype(jnp.float32)[:, None, :, None])
    return q, k, v, g


def zero_state(b):
    return (jnp.zeros((b, N_H, DH, DH), jnp.float32), jnp.zeros((b, N_H, DH), jnp.float32),
            jnp.zeros((b, N_H), jnp.float32))


def mlstm_scan(q, k, v, ig, fl, state):
    b, h, l, dh = q.shape
    nc = l // CHUNK

    def chunks(a):
        return jnp.moveaxis(a.reshape((b, h, nc, CHUNK) + a.shape[3:]), 2, 0)

    tril = jnp.tril(jnp.ones((CHUNK, CHUNK), dtype=bool))

    def step(carry, inp):
        c0, n0, m0 = carry
        qc, kc, vc, ic, fc = inp
        cum = jnp.cumsum(fc, axis=-1)
        logw = jnp.where(tril, cum[..., :, None] - cum[..., None, :] + ic[..., None, :], -jnp.inf)
        inter = cum + m0[..., None]
        m = jnp.maximum(inter, logw.max(-1))
        s = jnp.einsum('bhjd,bhsd->bhjs', qc, kc) * jnp.exp(logw - m[..., None])
        dec = jnp.exp(inter - m)
        num = jnp.einsum('bhjs,bhsd->bhjd', s, vc) + dec[..., None] * jnp.einsum('bhjk,bhkv->bhjv', qc, c0)
        den = s.sum(-1) + dec * jnp.einsum('bhjk,bhk->bhj', qc, n0)
        hc = num / jnp.maximum(jnp.abs(den), jnp.exp(-m))[..., None]
        tot = cum[..., -1]
        a = tot[..., None] - cum + ic
        m_new = jnp.maximum(tot + m0, a.max(-1))
        wa = jnp.exp(a - m_new[..., None])
        dn = jnp.exp(tot + m0 - m_new)
        c_new = dn[..., None, None] * c0 + jnp.einsum('bhs,bhsk,bhsv->bhkv', wa, kc, vc)
        n_new = dn[..., None] * n0 + jnp.einsum('bhs,bhsk->bhk', wa, kc)
        return (c_new, n_new, m_new), hc

    final, hs = lax.scan(step, state, (chunks(q), chunks(k), chunks(v), chunks(ig), chunks(fl)))
    return jnp.moveaxis(hs, 0, 2).reshape(b, h, l, dh), final


def bidir_mlstm(q, k, v, g, init):
    flip = lambda a: jnp.flip(a, axis=2)
    h_f, s_f = mlstm_scan(q, k, v, g[0], jax.nn.log_sigmoid(g[1]), init[0])
    h_b, s_b = mlstm_scan(flip(q), flip(k), flip(v), flip(g[2]), flip(jax.nn.log_sigmoid(g[3])), init[1])
    return h_f + flip(h_b), (s_f, s_b)


def branch_merge(p, h_m, w_fourier, w_mlstm, w_out):
    b, l, _ = p.shape
    u = p[..., OFF_F:OFF_QK].astype(jnp.float32).reshape(b, l, N_FG, FG)
    fu = jnp.fft.fft2(u, axes=(1, 3), norm='ortho').real.astype(p.dtype).reshape(b, l, D_F)
    y_f = fu @ w_fourier
    hm = h_m.transpose(0, 2, 1, 3).reshape(b, l, D_M).astype(p.dtype)
    y_m = (jax.nn.sigmoid(p[..., OFF_O:OFF_G]) * hm) @ w_mlstm
    gates = jax.nn.sigmoid(p[..., OFF_BR:])
    merged = gates[..., :D_MODEL] * y_f + gates[..., D_MODEL:] * y_m
    return merged @ w_out


def expert_choice(h, w_router, w_gate, w_up, w_down):
    b, l, _ = h.shape
    cap = CAP_FACTOR * l // N_EXPERTS
    aff = jax.nn.softmax(jnp.einsum('bld,de->ble', h, w_router).astype(jnp.float32), axis=-1)
    vals, idx = lax.top_k(jnp.swapaxes(aff, 1, 2), cap)
    bidx = jnp.arange(b)[:, None, None]
    xs = h[bidx, idx]
    hid = jax.nn.silu(jnp.einsum('becd,edf->becf', xs, w_gate)) * jnp.einsum('becd,edf->becf', xs, w_up)
    y = jnp.einsum('becf,efd->becd', hid, w_down) * vals[..., None].astype(h.dtype)
    return jnp.zeros_like(h).at[bidx, idx].add(y)


def setup_inputs(seed: int = 0) -> dict:
    key = jax.random.key(seed)
    ks = jax.random.split(key, 20)
    nrm = jax.random.normal
    forget = jnp.linspace(FORGET_BIAS_LO, FORGET_BIAS_HI, N_H)
    zeros_h = jnp.zeros((N_H,))
    gate_base = jnp.stack([zeros_h, forget, zeros_h, forget])
    return {
        'x': nrm(ks[0], (BATCH, SEQ, D_MODEL), jnp.float32),
        'c': nrm(ks[1], (BATCH, D_MODEL), jnp.float32),
        'ctx': nrm(ks[2], (BATCH, CTX_LEN, D_MODEL), jnp.float32),
        'c_ctx': nrm(ks[3], (D_MODEL,), jnp.float32),
        'w_ada': nrm(ks[4], (DEPTH, D_MODEL, 6 * D_MODEL), jnp.float32) * (0.5 * D_MODEL ** -0.5),
        'b_ada': nrm(ks[5], (DEPTH, 6 * D_MODEL), jnp.float32) * 0.02,
        'g_norm1': 1.0 + 0.02 * nrm(ks[6], (DEPTH, D_MODEL), jnp.float32),
        'w_in': nrm(ks[7], (DEPTH, D_MODEL, IN_COLS), jnp.float32) * D_MODEL ** -0.5,
        'b_gates': gate_base[None] + 0.1 * nrm(ks[8], (DEPTH, N_GATE_KINDS, N_H), jnp.float32),
        'w_conv': nrm(ks[9], (DEPTH, CONV_K, CONV_K, 2 * D_M), jnp.float32) * (1.0 / CONV_K),
        'w_fourier': nrm(ks[10], (DEPTH, D_F, D_MODEL), jnp.float32) * D_F ** -0.5,
        'w_mlstm': nrm(ks[11], (DEPTH, D_M, D_MODEL), jnp.float32) * D_M ** -0.5,
        'w_out': nrm(ks[12], (DEPTH, D_MODEL, D_MODEL), jnp.float32) * D_MODEL ** -0.5,
        'g_norm2': 1.0 + 0.02 * nrm(ks[13], (DEPTH, D_MODEL), jnp.float32),
        'w_router': nrm(ks[14], (DEPTH, D_MODEL, N_EXPERTS), jnp.float32) * D_MODEL ** -0.5,
        'w_gate_e': nrm(ks[15], (DEPTH, N_EXPERTS, D_MODEL, D_FF_E), jnp.float32) * D_MODEL ** -0.5,
        'w_up_e': nrm(ks[16], (DEPTH, N_EXPERTS, D_MODEL, D_FF_E), jnp.float32) * D_MODEL ** -0.5,
        'w_down_e': nrm(ks[17], (DEPTH, N_EXPERTS, D_FF_E, D_MODEL), jnp.float32) * D_FF_E ** -0.5,
        'g_final': 1.0 + 0.02 * nrm(ks[18], (D_MODEL,), jnp.float32),
    }


def reference(x, c, ctx, c_ctx, w_ada, b_ada, g_norm1, w_in, b_gates, w_conv, w_fourier, w_mlstm,
              w_out, g_norm2, w_router, w_gate_e, w_up_e, w_down_e, g_final):
    b, seq, _ = x.shape
    rows = seq // GRID_W
    for l in range(DEPTH):
        last = l == DEPTH - 1
        mx = (jax.nn.silu(c) @ w_ada[l] + b_ada[l])[:, None, :]
        sh1, sc1, ga1, sh2, sc2, ga2 = jnp.split(mx, 6, axis=-1)
        mc = jax.nn.silu(c_ctx) @ w_ada[l] + b_ada[l]
        csh1, csc1, cga1, csh2, csc2, cga2 = jnp.split(mc, 6, axis=-1)
        pc = modulate(rmsnorm(ctx, g_norm1[l]), csh1, csc1) @ w_in[l]
        qc, kc, vc, gc = mlstm_inputs(pc, 1, w_conv[l], b_gates[l])
        h_mc, ctx_states = bidir_mlstm(qc, kc, vc, gc, (zero_state(b), zero_state(b)))
        px = modulate(rmsnorm(x, g_norm1[l]), sh1, sc1) @ w_in[l]
        qx, kx, vx, gx = mlstm_inputs(px, rows, w_conv[l], b_gates[l])
        h_mx, _ = bidir_mlstm(qx, kx, vx, gx, ctx_states)
        x = x + ga1 * branch_merge(px, h_mx, w_fourier[l], w_mlstm[l], w_out[l])
        hx2 = modulate(rmsnorm(x, g_norm2[l]), sh2, sc2)
        x = x + ga2 * expert_choice(hx2, w_router[l], w_gate_e[l], w_up_e[l], w_down_e[l])
        if not last:
            ctx = ctx + cga1 * branch_merge(pc, h_mc, w_fourier[l], w_mlstm[l], w_out[l])
            hc2 = modulate(rmsnorm(ctx, g_norm2[l]), csh2, csc2)
            ctx = ctx + cga2 * expert_choice(hc2, w_router[l], w_gate_e[l], w_up_e[l], w_down_e[l])
    return rmsnorm(x, g_final)
```

```python
import functools
import math

import jax
import jax.numpy as jnp
from jax import lax
from jax.experimental import pallas as pl
from jax.experimental.pallas import tpu as pltpu

F32 = jnp.float32
BF16 = jnp.bfloat16

GRID_W = 64
N_FG = 4
N_H = 4
N_GATE_KINDS = 4
N_EXPERTS = 16
CAP_FACTOR = 2
EPS = 1e-6
N_MOD = 6

LANES = 128
V7X_VMEM_BYTES = 64 * 1024 * 1024
VMEM_LIMIT = V7X_VMEM_BYTES - 8 * 1024 * 1024

SCAN_CHUNK = 256
N_SCAN_ROWS = 8
N_TIE_EXTRACT = 3


def _params(*sem):
    return pltpu.CompilerParams(dimension_semantics=sem, vmem_limit_bytes=VMEM_LIMIT)


def _silu(v):
    return v * jax.nn.sigmoid(v)


def _rms_mod(v, g, shift, scale):
    ms = jnp.mean(v * v, axis=-1, keepdims=True)
    return (v * lax.rsqrt(ms + EPS) * g) * (1.0 + scale) + shift


def _ada_kernel(c_ref, w_ref, b_ref, o_ref):
    s = _silu(c_ref[...])
    o_ref[...] = jnp.dot(s, w_ref[...], preferred_element_type=F32,
                         precision=lax.Precision.HIGHEST) + b_ref[...]


def _ada(c_all, w_ada, b_ada):
    n, d = c_all.shape
    cols = w_ada.shape[1]
    tn = 512
    return pl.pallas_call(
        _ada_kernel,
        out_shape=jax.ShapeDtypeStruct((n, cols), F32),
        grid=(cols // tn,),
        in_specs=[pl.BlockSpec((n, d), lambda j: (0, 0)),
                  pl.BlockSpec((d, tn), lambda j: (0, j)),
                  pl.BlockSpec((1, tn), lambda j: (0, j))],
        out_specs=pl.BlockSpec((n, tn), lambda j: (0, j)),
        compiler_params=_params("parallel"),
        name="ada",
    )(c_all, w_ada, b_ada.reshape(1, cols))


def _inproj_kernel(x_ref, mod_ref, g_ref, *refs, n_seg):
    w_refs = refs[:n_seg]
    wgt_ref = refs[n_seg]
    o_refs = refs[n_seg + 1:2 * n_seg + 1]
    gt_ref = refs[2 * n_seg + 1]
    h = _rms_mod(x_ref[0], g_ref[...], mod_ref[0, 0:1, :], mod_ref[0, 1:2, :])
    hb = h.astype(BF16)
    for w_ref, o_ref in zip(w_refs, o_refs):
        o_ref[0] = jnp.dot(hb, w_ref[...], preferred_element_type=F32).astype(o_ref.dtype)
    gt_ref[0] = lax.dot_general(wgt_ref[...], hb, (((1,), (1,)), ((), ())),
                                preferred_element_type=F32)


def _inproj(x, mods, mod_row, g, w_segs, w_gt, tm):
    b, l, d = x.shape
    n_seg = len(w_segs)
    ng = w_gt.shape[0]
    const2 = lambda bi, i: (0, 0)
    in_specs = [pl.BlockSpec((1, tm, d), lambda bi, i: (bi, i, 0)),
                pl.BlockSpec((1, N_MOD, d), mod_row),
                pl.BlockSpec((1, d), const2)]
    in_specs += [pl.BlockSpec(w.shape, const2, pipeline_mode=pl.Buffered(1)) for w in w_segs]
    in_specs += [pl.BlockSpec(w_gt.shape, const2)]
    out_shape = [jax.ShapeDtypeStruct((b, l, w.shape[1]), BF16) for w in w_segs]
    out_shape += [jax.ShapeDtypeStruct((b, ng, l), F32)]
    out_specs = [pl.BlockSpec((1, tm, w.shape[1]), lambda bi, i: (bi, i, 0)) for w in w_segs]
    out_specs += [pl.BlockSpec((1, ng, tm), lambda bi, i: (bi, 0, i))]
    return pl.pallas_call(
        functools.partial(_inproj_kernel, n_seg=n_seg),
        out_shape=out_shape,
        grid=(b, l // tm),
        in_specs=in_specs,
        out_specs=out_specs,
        compiler_params=_params("parallel", "parallel"),
        name="inproj",
    )(x, mods, g.reshape(1, d), *w_segs, w_gt)


def _conv_kernel(p_ref, w_ref, s_ref, o_ref, *, rows, width):
    v = p_ref[0].astype(F32)
    l = v.shape[0]
    pos = lax.broadcasted_iota(jnp.int32, v.shape, 0)
    col = pos % width
    vl = jnp.where(col == 0, 0.0, pltpu.roll(v, 1, 0))
    vr = jnp.where(col == width - 1, 0.0, pltpu.roll(v, l - 1, 0))
    w = w_ref[...]

    def tap(dr):
        return (vl * w[3 * dr:3 * dr + 1, :] + v * w[3 * dr + 1:3 * dr + 2, :]
                + vr * w[3 * dr + 2:3 * dr + 3, :])

    acc = tap(1)
    if rows > 1:
        acc = acc + jnp.where(pos < width, 0.0, pltpu.roll(tap(0), width, 0))
        acc = acc + jnp.where(pos >= l - width, 0.0, pltpu.roll(tap(2), l - width, 0))
    o_ref[0] = (_silu(acc) * s_ref[...]).astype(o_ref.dtype)


def _conv(pqk, w9, scale, rows):
    b, l, ch = pqk.shape
    ct = 256
    return pl.pallas_call(
        functools.partial(_conv_kernel, rows=rows, width=l // rows),
        out_shape=jax.ShapeDtypeStruct((b, l, ch), BF16),
        grid=(b, ch // ct),
        in_specs=[pl.BlockSpec((1, l, ct), lambda bi, j: (bi, 0, j)),
                  pl.BlockSpec((9, ct), lambda bi, j: (0, j)),
                  pl.BlockSpec((1, ct), lambda bi, j: (0, j))],
        out_specs=pl.BlockSpec((1, l, ct), lambda bi, j: (bi, 0, j)),
        compiler_params=_params("parallel", "parallel"),
        name="conv",
    )(pqk, w9, scale)


def _log_sigmoid(v):
    return jnp.minimum(v, 0.0) - jnp.log1p(jnp.exp(-jnp.abs(v)))


def _chunk_scan(v, op, ident, chunk, reverse):
    l = v.shape[1]
    pos = lax.broadcasted_iota(jnp.int32, v.shape, 1) % chunk
    s = 1
    while s < chunk:
        if reverse:
            shifted = jnp.where(pos < chunk - s, pltpu.roll(v, l - s, 1), ident)
        else:
            shifted = jnp.where(pos >= s, pltpu.roll(v, s, 1), ident)
        v = op(v, shifted)
        s *= 2
    return v


def _gateprep_kernel(g_ref, b_ref, o_ref, *, chunk):
    g = g_ref[0] + b_ref[...]
    i_f = g[0:N_H]
    f_f = _log_sigmoid(g[N_H:2 * N_H])
    i_b = g[2 * N_H:3 * N_H]
    f_b = _log_sigmoid(g[3 * N_H:4 * N_H])
    neg = -jnp.inf
    cum_f = _chunk_scan(f_f, jnp.add, 0.0, chunk, False)
    br_f = i_f - cum_f
    pm_f = _chunk_scan(br_f, jnp.maximum, neg, chunk, False)
    cum_b = _chunk_scan(f_b, jnp.add, 0.0, chunk, True)
    br_b = i_b - cum_b
    pm_b = _chunk_scan(br_b, jnp.maximum, neg, chunk, True)
    zero = jnp.zeros_like(cum_f[0:1])
    for h in range(N_H):
        o_ref[0, h] = jnp.concatenate(
            [a[h:h + 1] for a in (cum_f, br_f, pm_f, cum_b, br_b, pm_b)] + [zero, zero], axis=0)


def _gateprep(g_t, b_gates, chunk):
    b, ng, l = g_t.shape
    return pl.pallas_call(
        functools.partial(_gateprep_kernel, chunk=chunk),
        out_shape=jax.ShapeDtypeStruct((b, N_H, N_SCAN_ROWS, l), F32),
        grid=(b,),
        in_specs=[pl.BlockSpec((1, ng, l), lambda bi: (bi, 0, 0)),
                  pl.BlockSpec((ng, 1), lambda bi: (0, 0))],
        out_specs=pl.BlockSpec((1, N_H, N_SCAN_ROWS, l), lambda bi: (bi, 0, 0, 0)),
        compiler_params=_params("parallel"),
        name="gateprep",
    )(g_t, b_gates.reshape(ng, 1))


def _scan_kernel(*refs, chunk, want_h, want_state):
    q_ref, k_ref, v_ref = refs[:3]
    pos = 3
    if want_h:
        og_ref = refs[pos]
        pos += 1
    rr_ref, rc_ref, c0_ref, m0_ref = refs[pos:pos + 4]
    pos += 4
    if want_h:
        h_ref = refs[pos]
        pos += 1
    if want_state:
        cs_ref, ms_ref = refs[pos:pos + 2]
        pos += 2
    vext_scr = refs[pos]
    if want_h:
        hf_scr, hb_scr = refs[pos + 1:pos + 3]

    l, dh = q_ref.shape[1], q_ref.shape[2]
    nc = l // chunk
    lane = lax.broadcasted_iota(jnp.int32, (l, dh), 1)
    vext_scr[:, 0:dh] = v_ref[0]
    vext_scr[:, dh:2 * dh] = jnp.where(lane == 0, 1.0, 0.0).astype(BF16)

    jj = lax.broadcasted_iota(jnp.int32, (chunk, chunk), 0)
    ss = lax.broadcasted_iota(jnp.int32, (chunk, chunk), 1)
    causal = (jj >= ss, jj <= ss)

    cst = [c0_ref[0, d, 0] for d in range(2)]
    mst = [m0_ref[0, d, 0][:, 0:1] for d in range(2)]

    for i in range(nc):
        for d in range(2):
            c = i if d == 0 else nc - 1 - i
            lo = c * chunk
            last = lo + (chunk - 1 if d == 0 else 0)
            r0 = 3 * d
            qc = q_ref[0, lo:lo + chunk, :]
            kc = k_ref[0, lo:lo + chunk, :]
            vc = vext_scr[lo:lo + chunk, :]
            cum_c = rc_ref[0, 0, lo:lo + chunk, r0:r0 + 1]
            br_c = rc_ref[0, 0, lo:lo + chunk, r0 + 1:r0 + 2]
            pm_c = rc_ref[0, 0, lo:lo + chunk, r0 + 2:r0 + 3]
            br_r = rr_ref[0, 0, r0 + 1:r0 + 2, lo:lo + chunk]
            tot = rr_ref[0, 0, r0:r0 + 1, last:last + 1]
            pm_last = rr_ref[0, 0, r0 + 2:r0 + 3, last:last + 1]
            m0 = mst[d]
            mm = jnp.maximum(m0, pm_c)
            decay = jnp.where(causal[d], jnp.exp(br_r - mm), 0.0)
            s = lax.dot_general(qc, kc, (((1,), (1,)), ((), ())), preferred_element_type=F32)
            p = (s * decay).astype(BF16)
            intra = jnp.dot(p, vc, preferred_element_type=F32)
            inter = jnp.dot(qc, cst[d].astype(BF16), preferred_element_type=F32)
            both = intra + jnp.exp(m0 - mm) * inter
            if want_h:
                den = both[:, dh:dh + 1]
                bound = jnp.exp(-(cum_c + mm))
                hc = both[:, 0:dh] / jnp.maximum(jnp.abs(den), bound)
                (hf_scr if d == 0 else hb_scr)[lo:lo + chunk, :] = hc
            mm_last = jnp.maximum(m0, pm_last)
            kw = (kc.astype(F32) * jnp.exp(br_c - mm_last)).astype(BF16)
            upd = lax.dot_general(kw, vc, (((0,), (0,)), ((), ())), preferred_element_type=F32)
            cst[d] = jnp.exp(m0 - mm_last) * cst[d] + upd
            mst[d] = tot + mm_last

    if want_h:
        gate = jax.nn.sigmoid(og_ref[0].astype(F32))
        h_ref[0] = (gate * (hf_scr[...] + hb_scr[...])).astype(h_ref.dtype)
    if want_state:
        for d in range(2):
            cs_ref[0, d, 0] = cst[d]
            ms_ref[0, d, 0] = jnp.broadcast_to(mst[d], (1, LANES))


def _scan(qk, v, og, rr, rc, c0, m0, *, want_h, want_state):
    b, l, dm = v.shape
    dh = dm // N_H
    chunk = min(SCAN_CHUNK, l)
    tok = lambda off: pl.BlockSpec((1, l, dh), lambda bi, h: (bi, 0, h + off))
    in_specs = [tok(0), tok(N_H), tok(0)]
    args = [qk, qk, v]
    if want_h:
        in_specs.append(tok(0))
        args.append(og)
    st_c = pl.BlockSpec((1, 2, 1, dh, 2 * dh), lambda bi, h: (bi, 0, h, 0, 0))
    st_m = pl.BlockSpec((1, 2, 1, 1, LANES), lambda bi, h: (bi, 0, h, 0, 0))
    in_specs += [pl.BlockSpec((1, 1, N_SCAN_ROWS, l), lambda bi, h: (bi, h, 0, 0)),
                 pl.BlockSpec((1, 1, l, N_SCAN_ROWS), lambda bi, h: (bi, h, 0, 0)),
                 st_c, st_m]
    args += [rr, rc, c0, m0]
    out_shape, out_specs = [], []
    scratch = [pltpu.VMEM((l, 2 * dh), BF16)]
    if want_h:
        out_shape.append(jax.ShapeDtypeStruct((b, l, dm), BF16))
        out_specs.append(tok(0))
        scratch += [pltpu.VMEM((l, dh), F32), pltpu.VMEM((l, dh), F32)]
    if want_state:
        out_shape += [jax.ShapeDtypeStruct(c0.shape, F32), jax.ShapeDtypeStruct(m0.shape, F32)]
        out_specs += [st_c, st_m]
    return pl.pallas_call(
        functools.partial(_scan_kernel, chunk=chunk, want_h=want_h, want_state=want_state),
        out_shape=out_shape,
        grid=(b, N_H),
        in_specs=in_specs,
        out_specs=out_specs,
        scratch_shapes=scratch,
        compiler_params=_params("parallel", "parallel"),
        name="scan_h" if want_h else "scan_state",
    )(*args)


def _fourier_kernel(u_ref, wc_ref, m2_ref, o_ref, y_scr, *, fg):
    l = u_ref.shape[1]

    @pl.when(pl.program_id(1) == 0)
    def _():
        for g in range(u_ref.shape[2] // fg):
            r = jnp.dot(u_ref[0, :, g * fg:(g + 1) * fg], wc_ref[...], preferred_element_type=F32)
            y_scr[0:l, g * fg:(g + 1) * fg] = r[:, 0:fg].astype(BF16)
            y_scr[l:2 * l, g * fg:(g + 1) * fg] = r[:, fg:2 * fg].astype(BF16)

    o_ref[0] = jnp.dot(m2_ref[...], y_scr[...], preferred_element_type=F32).astype(o_ref.dtype)


def _dft_mats(n):
    idx = jnp.arange(n, dtype=jnp.int32)
    ang = ((idx[:, None] * idx[None, :]) % n).astype(F32) * (2.0 * math.pi / n)
    return jnp.cos(ang) * n ** -0.5, jnp.sin(ang) * n ** -0.5


def _fourier(u):
    b, l, df = u.shape
    fg = df // N_FG
    cc, sc = _dft_mats(fg)
    cl, sl = _dft_mats(l)
    wc = jnp.concatenate([cc, sc], axis=1).astype(BF16)
    m2 = jnp.concatenate([cl, -sl], axis=1).astype(BF16)
    tr = 512
    return pl.pallas_call(
        functools.partial(_fourier_kernel, fg=fg),
        out_shape=jax.ShapeDtypeStruct((b, l, df), BF16),
        grid=(b, l // tr),
        in_specs=[pl.BlockSpec((1, l, df), lambda bi, i: (bi, 0, 0)),
                  pl.BlockSpec((fg, 2 * fg), lambda bi, i: (0, 0)),
                  pl.BlockSpec((tr, 2 * l), lambda bi, i: (i, 0))],
        out_specs=pl.BlockSpec((1, tr, df), lambda bi, i: (bi, i, 0)),
        scratch_shapes=[pltpu.VMEM((2 * l, df), BF16)],
        compiler_params=_params("parallel", "arbitrary"),
        name="fourier",
    )(u, wc, m2)


def _merge_kernel(fu_ref, hm_ref, br_ref, x_ref, mod_ref, wf_ref, wm_ref, wo_ref, g2_ref,
                  wr1_ref, wr2_ref, x1_ref, hx_ref, aff_ref):
    d = x_ref.shape[2]
    ne = aff_ref.shape[2]
    yf = jnp.dot(fu_ref[0], wf_ref[...], preferred_element_type=F32)
    ym = jnp.dot(hm_ref[0], wm_ref[...], preferred_element_type=F32)
    gf = jax.nn.sigmoid(br_ref[0, :, 0:d].astype(F32))
    gm = jax.nn.sigmoid(br_ref[0, :, d:2 * d].astype(F32))
    merged = (gf * yf + gm * ym).astype(BF16)
    o = jnp.dot(merged, wo_ref[...], preferred_element_type=F32)
    x1 = x_ref[0] + mod_ref[0, 2:3, :] * o
    x1_ref[0] = x1
    h2 = _rms_mod(x1, g2_ref[...], mod_ref[0, 3:4, :], mod_ref[0, 4:5, :])
    hi = h2.astype(BF16)
    hx_ref[0] = hi
    lo = (h2 - hi.astype(F32)).astype(BF16)
    r = (jnp.dot(hi, wr1_ref[...], preferred_element_type=F32)
         + jnp.dot(lo, wr2_ref[...], preferred_element_type=F32))
    logits = r[:, 0:ne] + r[:, ne:2 * ne]
    e = jnp.exp(logits - jnp.max(logits, axis=-1, keepdims=True))
    aff_ref[0] = e / jnp.sum(e, axis=-1, keepdims=True)


def _merge(fu, hm, pbr, x, mods, wf, wm, wo, g2, w_router, tm):
    b, l, d = x.shape
    ne = w_router.shape[1]
    whi = w_router.astype(BF16)
    wlo = (w_router - whi.astype(F32)).astype(BF16)
    pad = jnp.zeros((d, LANES - 2 * ne), BF16)
    wr1 = jnp.concatenate([whi, wlo, pad], axis=1)
    wr2 = jnp.concatenate([whi, jnp.zeros((d, ne), BF16), pad], axis=1)
    tok = lambda n: pl.BlockSpec((1, tm, n), lambda bi, i: (bi, i, 0))
    const = lambda a: pl.BlockSpec(a.shape, lambda bi, i: (0, 0), pipeline_mode=pl.Buffered(1))
    return pl.pallas_call(
        _merge_kernel,
        out_shape=[jax.ShapeDtypeStruct((b, l, d), F32),
                   jax.ShapeDtypeStruct((b, l, d), BF16),
                   jax.ShapeDtypeStruct((b, l, ne), F32)],
        grid=(b, l // tm),
        in_specs=[tok(fu.shape[2]), tok(hm.shape[2]), tok(pbr.shape[2]), tok(d),
                  pl.BlockSpec((1, N_MOD, d), lambda bi, i: (bi, 0, 0)),
                  const(wf), const(wm), const(wo),
                  pl.BlockSpec((1, d), lambda bi, i: (0, 0)),
                  const(wr1), const(wr2)],
        out_specs=[tok(d), tok(d), tok(ne)],
        compiler_params=_params("parallel", "parallel"),
        name="merge",
    )(fu, hm, pbr, x, mods, wf, wm, wo, g2.reshape(1, d), wr1, wr2)


def _topk_kernel(a_ref, u_ref, slot_ref, *, k):
    a = a_ref[...]
    l = a.shape[1]
    total = lambda m: jnp.sum(m, axis=1, keepdims=True)
    ge = lambda bits: jnp.where(a >= pltpu.bitcast(bits, F32), 1.0, 0.0)

    def body(i, t):
        cand = t | lax.shift_left(jnp.int32(1), 30 - i)
        return jnp.where(total(ge(cand)) >= k, cand, t)

    t = lax.fori_loop(0, 31, body, jnp.zeros((a.shape[0], 1), jnp.int32))
    chosen = ge(t + 1)
    rem = ge(t) - chosen
    need = k - total(chosen)
    idx = lax.broadcasted_iota(jnp.int32, a.shape, 1).astype(F32)
    for _ in range(N_TIE_EXTRACT):
        cur = jnp.where(rem > 0.0, a, -1.0)
        is_mx = jnp.where(cur == jnp.max(cur, axis=1, keepdims=True), rem, 0.0)
        first = jnp.min(jnp.where(is_mx > 0.0, idx, float(l)), axis=1, keepdims=True)
        pick = jnp.where(idx == first, jnp.where(need > 0.0, 1.0, 0.0), 0.0)
        chosen = chosen + pick
        rem = rem - pick
        need = need - total(pick)
    before = jnp.dot(rem.astype(BF16), u_ref[...], preferred_element_type=F32)
    sel = chosen + jnp.where(before < need, rem, 0.0)
    rank = jnp.dot(sel.astype(BF16), u_ref[...], preferred_element_type=F32)
    slot_ref[...] = jnp.where(sel > 0.0, rank, -1.0)


def _topk(aff_t, k):
    r, l = aff_t.shape
    tr = min(128, r)
    idx = jnp.arange(l, dtype=jnp.int32)
    upper = (idx[:, None] < idx[None, :]).astype(BF16)
    return pl.pallas_call(
        functools.partial(_topk_kernel, k=k),
        out_shape=jax.ShapeDtypeStruct((r, l), F32),
        grid=(r // tr,),
        in_specs=[pl.BlockSpec((tr, l), lambda i: (i, 0)),
                  pl.BlockSpec((l, l), lambda i: (0, 0), pipeline_mode=pl.Buffered(1))],
        out_specs=pl.BlockSpec((tr, l), lambda i: (i, 0)),
        compiler_params=_params("parallel"),
        name="topk",
    )(aff_t, upper)


def _moe_kernel(hx_ref, srow_ref, arow_ref, scol_ref, wg_ref, wu_ref, wd_ref, o_ref, *, cap):
    e = pl.program_id(1)
    l = hx_ref.shape[1]
    ne = scol_ref.shape[2]
    slot_i = lax.broadcasted_iota(jnp.int32, (cap, l), 0).astype(F32)
    hit = srow_ref[0, 0] == slot_i
    xs = jnp.dot(jnp.where(hit, 1.0, 0.0).astype(BF16), hx_ref[0],
                 preferred_element_type=F32).astype(BF16)
    gate = jnp.dot(xs, wg_ref[0], preferred_element_type=F32)
    up = jnp.dot(xs, wu_ref[0], preferred_element_type=F32)
    hid = (_silu(gate) * up).astype(BF16)
    y = jnp.dot(hid, wd_ref[0], preferred_element_type=F32)
    vals = jnp.sum(jnp.where(hit, arow_ref[0, 0], 0.0), axis=1, keepdims=True)
    yb = (y * vals).astype(BF16)
    lane_e = lax.broadcasted_iota(jnp.int32, (l, ne), 1)
    scol = jnp.sum(jnp.where(lane_e == e, scol_ref[0], 0.0), axis=1, keepdims=True)
    slot_j = lax.broadcasted_iota(jnp.int32, (l, cap), 1).astype(F32)
    pt = jnp.where(scol == slot_j, 1.0, 0.0).astype(BF16)
    contrib = jnp.dot(pt, yb, preferred_element_type=F32)

    @pl.when(e == 0)
    def _():
        o_ref[0] = contrib

    @pl.when(e > 0)
    def _():
        o_ref[0] += contrib


def _moe(hx, slot_row, aff_row, slot_col, wg, wu, wd, cap):
    b, l, d = hx.shape
    ne, _, f = wg.shape
    row = pl.BlockSpec((1, 1, 1, l), lambda bi, e: (bi, e, 0, 0))
    return pl.pallas_call(
        functools.partial(_moe_kernel, cap=cap),
        out_shape=jax.ShapeDtypeStruct((b, l, d), F32),
        grid=(b, ne),
        in_specs=[pl.BlockSpec((1, l, d), lambda bi, e: (bi, 0, 0)),
                  row, row,
                  pl.BlockSpec((1, l, ne), lambda bi, e: (bi, 0, 0)),
                  pl.BlockSpec((1, d, f), lambda bi, e: (e, 0, 0)),
                  pl.BlockSpec((1, d, f), lambda bi, e: (e, 0, 0)),
                  pl.BlockSpec((1, f, d), lambda bi, e: (e, 0, 0))],
        out_specs=pl.BlockSpec((1, l, d), lambda bi, e: (bi, 0, 0)),
        compiler_params=_params("parallel", "arbitrary"),
        name="moe",
    )(hx, slot_row, aff_row, slot_col, wg, wu, wd)


def _final_kernel(x1_ref, moe_ref, mod_ref, g_ref, o_ref):
    v = x1_ref[0] + mod_ref[0, 5:6, :] * moe_ref[0]
    ms = jnp.mean(v * v, axis=-1, keepdims=True)
    o_ref[0] = v * lax.rsqrt(ms + EPS) * g_ref[...]


def _final(x1, moe, mods, g, tm):
    b, l, d = x1.shape
    tok = pl.BlockSpec((1, tm, d), lambda bi, i: (bi, i, 0))
    return pl.pallas_call(
        _final_kernel,
        out_shape=jax.ShapeDtypeStruct((b, l, d), F32),
        grid=(b, l // tm),
        in_specs=[tok, tok, pl.BlockSpec((1, N_MOD, d), lambda bi, i: (bi, 0, 0)),
                  pl.BlockSpec((1, d), lambda bi, i: (0, 0))],
        out_specs=tok,
        compiler_params=_params("parallel", "parallel"),
        name="final",
    )(x1, moe, mods, g.reshape(1, d))


def _scan_tables(g_t, b_gates, chunk):
    rr = _gateprep(g_t, b_gates, chunk)
    return rr, jnp.swapaxes(rr, 2, 3)


def kernel(x, c, ctx, c_ctx, w_ada, b_ada, g_norm1, w_in, b_gates, w_conv, w_fourier, w_mlstm,
           w_out, g_norm2, w_router, w_gate_e, w_up_e, w_down_e, g_final):
    assert w_ada.shape[0] == 1, "single-layer stack"
    b, l, d = x.shape
    lc = ctx.shape[1]
    d_f = d // 2
    d_m = d // 2
    dh = d_m // N_H
    ne = w_router.shape[2]
    cap = CAP_FACTOR * l // ne
    off_qk = d_f
    off_v = off_qk + 2 * d_m
    off_o = off_v + d_m
    off_g = off_o + d_m
    off_br = off_g + N_GATE_KINDS * N_H

    n_c = b + 1
    n_pad = -n_c % 8
    c_all = jnp.concatenate([c, c_ctx[None, :], jnp.zeros((n_pad, d), F32)], axis=0)
    mods = _ada(c_all, w_ada[0], b_ada[0]).reshape(n_c + n_pad, N_MOD, d)

    w = w_in[0]
    wb = w.astype(BF16)
    w_f, w_qk, w_v, w_o, w_br = (wb[:, 0:off_qk], wb[:, off_qk:off_v], wb[:, off_v:off_o],
                                 wb[:, off_o:off_g], wb[:, off_br:])
    w_gt = wb[:, off_g:off_br].T
    w9 = w_conv[0].reshape(9, 2 * d_m)
    qk_scale = jnp.concatenate([jnp.ones((1, d_m), F32), jnp.full((1, d_m), dh ** -0.5, F32)], axis=1)

    pc_qk, pc_v, gc_t = _inproj(ctx, mods, lambda bi, i: (b, 0, 0), g_norm1[0],
                                [w_qk, w_v], w_gt, tm=lc)
    qk_c = _conv(pc_qk, w9, qk_scale, rows=1)
    rr_c, rc_c = _scan_tables(gc_t, b_gates[0], min(SCAN_CHUNK, lc))
    c0 = jnp.zeros((b, 2, N_H, dh, 2 * dh), F32)
    m0 = jnp.zeros((b, 2, N_H, 1, LANES), F32)
    c_seed, m_seed = _scan(qk_c, pc_v, None, rr_c, rc_c, c0, m0, want_h=False, want_state=True)

    p_f, p_qk, p_v, p_o, p_br, g_t = _inproj(x, mods, lambda bi, i: (bi, 0, 0), g_norm1[0],
                                             [w_f, w_qk, w_v, w_o, w_br], w_gt, tm=512)
    qk_x = _conv(p_qk, w9, qk_scale, rows=l // GRID_W)
    rr_x, rc_x = _scan_tables(g_t, b_gates[0], SCAN_CHUNK)
    (hm,) = _scan(qk_x, p_v, p_o, rr_x, rc_x, c_seed, m_seed, want_h=True, want_state=False)
    fu = _fourier(p_f)
    x1, hx2, aff = _merge(fu, hm, p_br, x, mods, w_fourier[0].astype(BF16), w_mlstm[0].astype(BF16),
                          w_out[0].astype(BF16), g_norm2[0], w_router[0], tm=512)

    aff_t = jnp.swapaxes(aff, 1, 2)
    slot_t = _topk(aff_t.reshape(b * ne, l), cap).reshape(b, ne, l)
    moe = _moe(hx2, slot_t.reshape(b, ne, 1, l), aff_t.reshape(b, ne, 1, l),
               jnp.swapaxes(slot_t, 1, 2), w_gate_e[0].astype(BF16), w_up_e[0].astype(BF16),
               w_down_e[0].astype(BF16), cap)
    return _final(x1, moe, mods, g_final, tm=512)
```

```python
import functools
import math

import jax
import jax.numpy as jnp
from jax import lax
from jax.experimental import pallas as pl
from jax.experimental.pallas import tpu as pltpu

F32 = jnp.float32
BF16 = jnp.bfloat16

GRID_W = 64
N_FG = 4
N_H = 4
N_GATE_KINDS = 4
N_EXPERTS = 16
CAP_FACTOR = 2
EPS = 1e-6
LOG2E = math.log2(math.e)
N_MOD = 6

LANES = 128
V7X_VMEM_BYTES = 64 * 1024 * 1024
VMEM_LIMIT = V7X_VMEM_BYTES - 8 * 1024 * 1024

SCAN_CHUNK = 256
N_SCAN_ROWS = 8
N_TIE_EXTRACT = 3
MERGE_SUBTILE = 256


def _params(*sem):
    return pltpu.CompilerParams(dimension_semantics=sem, vmem_limit_bytes=VMEM_LIMIT)


def _silu(v):
    return v * jax.nn.sigmoid(v)


def _rms_mod(v, g, shift, scale):
    ms = jnp.mean(v * v, axis=-1, keepdims=True)
    return (v * lax.rsqrt(ms + EPS) * g) * (1.0 + scale) + shift


def _ada_kernel(c_ref, w_ref, b_ref, o_ref):
    s = _silu(c_ref[...])
    o_ref[...] = jnp.dot(s, w_ref[...], preferred_element_type=F32,
                         precision=lax.Precision.HIGHEST) + b_ref[...]


def _ada(c_all, w_ada, b_ada):
    n, d = c_all.shape
    cols = w_ada.shape[1]
    tn = 512
    return pl.pallas_call(
        _ada_kernel,
        out_shape=jax.ShapeDtypeStruct((n, cols), F32),
        grid=(cols // tn,),
        in_specs=[pl.BlockSpec((n, d), lambda j: (0, 0)),
                  pl.BlockSpec((d, tn), lambda j: (0, j)),
                  pl.BlockSpec((1, tn), lambda j: (0, j))],
        out_specs=pl.BlockSpec((n, tn), lambda j: (0, j)),
        compiler_params=_params("parallel"),
        name="ada",
    )(c_all, w_ada, b_ada.reshape(1, cols))


def _inproj_kernel(x_ref, mod_ref, g_ref, *refs, n_seg):
    w_refs = refs[:n_seg]
    wgt_ref = refs[n_seg]
    o_refs = refs[n_seg + 1:2 * n_seg + 1]
    gt_ref = refs[2 * n_seg + 1]
    h = _rms_mod(x_ref[0], g_ref[...], mod_ref[0, 0:1, :], mod_ref[0, 1:2, :])
    hb = h.astype(BF16)
    for w_ref, o_ref in zip(w_refs, o_refs):
        o_ref[0] = jnp.dot(hb, w_ref[...], preferred_element_type=F32).astype(o_ref.dtype)
    gt_ref[0] = lax.dot_general(wgt_ref[...], hb, (((1,), (1,)), ((), ())),
                                preferred_element_type=F32)


def _inproj(x, mods, mod_row, g, w_segs, w_gt, tm):
    b, l, d = x.shape
    n_seg = len(w_segs)
    ng = w_gt.shape[0]
    const2 = lambda bi, i: (0, 0)
    in_specs = [pl.BlockSpec((1, tm, d), lambda bi, i: (bi, i, 0)),
                pl.BlockSpec((1, N_MOD, d), mod_row),
                pl.BlockSpec((1, d), const2)]
    in_specs += [pl.BlockSpec(w.shape, const2, pipeline_mode=pl.Buffered(1)) for w in w_segs]
    in_specs += [pl.BlockSpec(w_gt.shape, const2)]
    out_shape = [jax.ShapeDtypeStruct((b, l, w.shape[1]), BF16) for w in w_segs]
    out_shape += [jax.ShapeDtypeStruct((b, ng, l), F32)]
    out_specs = [pl.BlockSpec((1, tm, w.shape[1]), lambda bi, i: (bi, i, 0)) for w in w_segs]
    out_specs += [pl.BlockSpec((1, ng, tm), lambda bi, i: (bi, 0, i))]
    return pl.pallas_call(
        functools.partial(_inproj_kernel, n_seg=n_seg),
        out_shape=out_shape,
        grid=(b, l // tm),
        in_specs=in_specs,
        out_specs=out_specs,
        compiler_params=_params("parallel", "parallel"),
        name="inproj",
    )(x, mods, g.reshape(1, d), *w_segs, w_gt)


def _conv_kernel(p_ref, w_ref, s_ref, o_ref, *, rows, width, transpose):
    v = p_ref[0].astype(F32)
    l = v.shape[0]
    pos = lax.broadcasted_iota(jnp.int32, v.shape, 0)
    col = pos % width
    vl = jnp.where(col == 0, 0.0, pltpu.roll(v, 1, 0))
    vr = jnp.where(col == width - 1, 0.0, pltpu.roll(v, l - 1, 0))
    w = w_ref[...]

    def tap(dr):
        return (vl * w[3 * dr:3 * dr + 1, :] + v * w[3 * dr + 1:3 * dr + 2, :]
                + vr * w[3 * dr + 2:3 * dr + 3, :])

    acc = tap(1)
    if rows > 1:
        acc = acc + jnp.where(pos < width, 0.0, pltpu.roll(tap(0), width, 0))
        acc = acc + jnp.where(pos >= l - width, 0.0, pltpu.roll(tap(2), l - width, 0))
    y = _silu(acc) * s_ref[...]
    o_ref[0] = (y.T if transpose else y).astype(o_ref.dtype)


def _conv(p, w9, scale, rows, col_lo, col_hi, transpose):
    b, l, _ = p.shape
    ct = 256
    j0 = col_lo // ct
    n = col_hi - col_lo
    if transpose:
        out_shape = jax.ShapeDtypeStruct((b, n, l), BF16)
        out_spec = pl.BlockSpec((1, ct, l), lambda bi, j: (bi, j, 0))
    else:
        out_shape = jax.ShapeDtypeStruct((b, l, n), BF16)
        out_spec = pl.BlockSpec((1, l, ct), lambda bi, j: (bi, 0, j))
    return pl.pallas_call(
        functools.partial(_conv_kernel, rows=rows, width=l // rows, transpose=transpose),
        out_shape=out_shape,
        grid=(b, n // ct),
        in_specs=[pl.BlockSpec((1, l, ct), lambda bi, j: (bi, 0, j + j0)),
                  pl.BlockSpec((9, ct), lambda bi, j: (0, j + j0)),
                  pl.BlockSpec((1, ct), lambda bi, j: (0, j + j0))],
        out_specs=out_spec,
        compiler_params=_params("parallel", "parallel"),
        name="conv_kt" if transpose else "conv_q",
    )(p, w9, scale)


def _log_sigmoid(v):
    return jnp.minimum(v, 0.0) - jnp.log1p(jnp.exp(-jnp.abs(v)))


def _chunk_scan(v, op, ident, chunk, reverse):
    l = v.shape[1]
    pos = lax.broadcasted_iota(jnp.int32, v.shape, 1) % chunk
    s = 1
    while s < chunk:
        if reverse:
            shifted = jnp.where(pos < chunk - s, pltpu.roll(v, l - s, 1), ident)
        else:
            shifted = jnp.where(pos >= s, pltpu.roll(v, s, 1), ident)
        v = op(v, shifted)
        s *= 2
    return v


def _gateprep_kernel(g_ref, b_ref, o_ref, *, chunk):
    g = g_ref[0] + b_ref[...]
    i_f = g[0:N_H] * LOG2E
    f_f = _log_sigmoid(g[N_H:2 * N_H]) * LOG2E
    i_b = g[2 * N_H:3 * N_H] * LOG2E
    f_b = _log_sigmoid(g[3 * N_H:4 * N_H]) * LOG2E
    neg = -jnp.inf
    cum_f = _chunk_scan(f_f, jnp.add, 0.0, chunk, False)
    br_f = i_f - cum_f
    pm_f = _chunk_scan(br_f, jnp.maximum, neg, chunk, False)
    cum_b = _chunk_scan(f_b, jnp.add, 0.0, chunk, True)
    br_b = i_b - cum_b
    pm_b = _chunk_scan(br_b, jnp.maximum, neg, chunk, True)
    zero = jnp.zeros_like(cum_f[0:1])
    for h in range(N_H):
        o_ref[0, h] = jnp.concatenate(
            [a[h:h + 1] for a in (cum_f, br_f, pm_f, cum_b, br_b, pm_b)] + [zero, zero], axis=0)


def _gateprep(g_t, b_gates, chunk):
    b, ng, l = g_t.shape
    return pl.pallas_call(
        functools.partial(_gateprep_kernel, chunk=chunk),
        out_shape=jax.ShapeDtypeStruct((b, N_H, N_SCAN_ROWS, l), F32),
        grid=(b,),
        in_specs=[pl.BlockSpec((1, ng, l), lambda bi: (bi, 0, 0)),
                  pl.BlockSpec((ng, 1), lambda bi: (0, 0))],
        out_specs=pl.BlockSpec((1, N_H, N_SCAN_ROWS, l), lambda bi: (bi, 0, 0, 0)),
        compiler_params=_params("parallel"),
        name="gateprep",
    )(g_t, b_gates.reshape(ng, 1))


def _scan_kernel(*refs, chunk, want_h, want_state):
    q_ref, kt_ref, v_ref = refs[:3]
    pos = 3
    if want_h:
        og_ref = refs[pos]
        pos += 1
    rr_ref, rc_ref, c0_ref, m0_ref = refs[pos:pos + 4]
    pos += 4
    if want_h:
        h_ref = refs[pos]
        pos += 1
    if want_state:
        cs_ref, ms_ref = refs[pos:pos + 2]
        pos += 2
    vext_scr = refs[pos]
    if want_h:
        hf_scr, hb_scr = refs[pos + 1:pos + 3]

    l, dh = q_ref.shape[1], q_ref.shape[2]
    nc = l // chunk
    rep = chunk // dh
    vext_scr[:, 0:dh] = v_ref[0]
    vext_scr[:, dh:2 * dh] = jnp.ones((l, dh), BF16)

    jj = lax.broadcasted_iota(jnp.int32, (chunk, chunk), 0)
    ss = lax.broadcasted_iota(jnp.int32, (chunk, chunk), 1)
    causal = (jj >= ss, jj <= ss)

    cst = [c0_ref[0, d, 0] for d in range(2)]
    mst = [m0_ref[0, d, 0][:, 0:1] for d in range(2)]

    for i in range(nc):
        for d in range(2):
            c = i if d == 0 else nc - 1 - i
            lo = c * chunk
            end = chunk - 1 if d == 0 else 0
            r0 = 3 * d
            qc = q_ref[0, lo:lo + chunk, :]
            ktc = kt_ref[0, :, lo:lo + chunk]
            vc = vext_scr[lo:lo + chunk, :]
            cum_c = rc_ref[0, 0, lo:lo + chunk, r0:r0 + 1]
            pm_c = rc_ref[0, 0, lo:lo + chunk, r0 + 2:r0 + 3]
            br_r = rr_ref[0, 0, r0 + 1:r0 + 2, lo:lo + chunk]
            tot = rr_ref[0, 0, r0:r0 + 1, lo + end:lo + end + 1]
            pm_end = rr_ref[0, 0, r0 + 2:r0 + 3, lo + end:lo + end + 1]
            m0 = mst[d]
            mm = jnp.broadcast_to(jnp.maximum(m0, pm_c), (chunk, dh))
            d_in = jnp.where(causal[d], jnp.exp2(br_r - jnp.concatenate([mm] * rep, axis=1)), 0.0)
            d_st = jnp.exp2(m0 - mm)
            s = jnp.dot(qc, ktc, preferred_element_type=F32)
            p = (s * d_in).astype(BF16)
            qd = (qc.astype(F32) * d_st).astype(BF16)
            both = (jnp.dot(p, vc, preferred_element_type=F32)
                    + jnp.dot(qd, cst[d].astype(BF16), preferred_element_type=F32))
            if want_h:
                m_j = jnp.broadcast_to(cum_c, (chunk, dh)) + mm
                hc = both[:, 0:dh] / jnp.maximum(jnp.abs(both[:, dh:2 * dh]), jnp.exp2(-m_j))
                (hf_scr if d == 0 else hb_scr)[lo:lo + chunk, :] = hc
            kw = (ktc.astype(F32) * d_in[end:end + 1, :]).astype(BF16)
            upd = jnp.dot(kw, vc, preferred_element_type=F32)
            dn = d_st[end:end + 1, :]
            cst[d] = jnp.concatenate([dn, dn], axis=1) * cst[d] + upd
            mst[d] = tot + jnp.maximum(m0, pm_end)

    if want_h:
        gate = jax.nn.sigmoid(og_ref[0].astype(F32))
        h_ref[0] = (gate * (hf_scr[...] + hb_scr[...])).astype(h_ref.dtype)
    if want_state:
        for d in range(2):
            cs_ref[0, d, 0] = cst[d]
            ms_ref[0, d, 0] = jnp.broadcast_to(mst[d], (1, LANES))


def _scan(q, kt, v, og, rr, rc, c0, m0, *, want_h, want_state):
    b, l, dm = v.shape
    dh = dm // N_H
    chunk = min(SCAN_CHUNK, l)
    assert chunk % dh == 0 and l % chunk == 0
    tok = pl.BlockSpec((1, l, dh), lambda bi, h: (bi, 0, h))
    in_specs = [tok, pl.BlockSpec((1, dh, l), lambda bi, h: (bi, h, 0)), tok]
    args = [q, kt, v]
    if want_h:
        in_specs.append(tok)
        args.append(og)
    st_c = pl.BlockSpec((1, 2, 1, dh, 2 * dh), lambda bi, h: (bi, 0, h, 0, 0))
    st_m = pl.BlockSpec((1, 2, 1, 1, LANES), lambda bi, h: (bi, 0, h, 0, 0))
    in_specs += [pl.BlockSpec((1, 1, N_SCAN_ROWS, l), lambda bi, h: (bi, h, 0, 0)),
                 pl.BlockSpec((1, 1, l, N_SCAN_ROWS), lambda bi, h: (bi, h, 0, 0)),
                 st_c, st_m]
    args += [rr, rc, c0, m0]
    out_shape, out_specs = [], []
    scratch = [pltpu.VMEM((l, 2 * dh), BF16)]
    if want_h:
        out_shape.append(jax.ShapeDtypeStruct((b, l, dm), BF16))
        out_specs.append(tok)
        scratch += [pltpu.VMEM((l, dh), F32), pltpu.VMEM((l, dh), F32)]
    if want_state:
        out_shape += [jax.ShapeDtypeStruct(c0.shape, F32), jax.ShapeDtypeStruct(m0.shape, F32)]
        out_specs += [st_c, st_m]
    return pl.pallas_call(
        functools.partial(_scan_kernel, chunk=chunk, want_h=want_h, want_state=want_state),
        out_shape=out_shape,
        grid=(b, N_H),
        in_specs=in_specs,
        out_specs=out_specs,
        scratch_shapes=scratch,
        compiler_params=_params("parallel", "parallel"),
        name="scan_h" if want_h else "scan_state",
    )(*args)


def _twiddle(z, k, n):
    zr, zi = z
    k %= n
    if k == 0:
        return z
    if 4 * k == n:
        return zi, -zr
    if 2 * k == n:
        return -zr, -zi
    if 4 * k == 3 * n:
        return -zi, zr
    wr, wi = math.cos(2.0 * math.pi * k / n), -math.sin(2.0 * math.pi * k / n)
    return zr * wr - zi * wi, zr * wi + zi * wr


def _fft_list(xs):
    n = len(xs)
    if n == 1:
        return xs
    ev, od = _fft_list(xs[0::2]), _fft_list(xs[1::2])
    out = [None] * n
    for k in range(n // 2):
        tr, ti = _twiddle(od[k], k, n)
        out[k] = (ev[k][0] + tr, ev[k][1] + ti)
        out[k + n // 2] = (ev[k][0] - tr, ev[k][1] - ti)
    return out


def _fourier_kernel(u_ref, wc_ref, tw_ref, cs_ref, o_ref, zz_scr, *, fg, n1):
    l, df = u_ref.shape[1], u_ref.shape[2]
    n2 = l // n1
    for g in range(df // fg):
        r = jnp.dot(u_ref[0, :, g * fg:(g + 1) * fg], wc_ref[...], preferred_element_type=F32)
        xs = [(r[j * n2:(j + 1) * n2, 0:fg], r[j * n2:(j + 1) * n2, fg:2 * fg]) for j in range(n1)]
        zs = _fft_list(xs)
        for k1 in range(n1):
            zr, zi = zs[k1]
            if k1 > 0:
                tc, ts = tw_ref[0, k1], tw_ref[1, k1]
                zr, zi = zr * tc + zi * ts, zi * tc - zr * ts
            zz_scr[k1, 0:n2, g * fg:(g + 1) * fg] = zr.astype(BF16)
            zz_scr[k1, n2:2 * n2, g * fg:(g + 1) * fg] = zi.astype(BF16)
    for k1 in range(n1):
        o_ref[0, :, k1 * df:(k1 + 1) * df] = jnp.dot(
            cs_ref[...], zz_scr[k1], preferred_element_type=F32).astype(o_ref.dtype)


def _dft_angles(rows, cols, n):
    ang = ((rows[:, None] * cols[None, :]) % n).astype(F32) * (2.0 * math.pi / n)
    return jnp.cos(ang), jnp.sin(ang)


def _fourier(u):
    b, l, df = u.shape
    fg = df // N_FG
    n1 = 8
    n2 = l // n1
    ar = lambda n: jnp.arange(n, dtype=jnp.int32)
    cc, sc = _dft_angles(ar(fg), ar(fg), fg)
    wc = (jnp.concatenate([cc, -sc], axis=1) * fg ** -0.5).astype(BF16)
    tc, ts = _dft_angles(ar(n1), ar(n2), l)
    tw = jnp.broadcast_to(jnp.stack([tc, ts])[..., None], (2, n1, n2, fg))
    c2, s2 = _dft_angles(ar(n2), ar(n2), n2)
    cs = (jnp.concatenate([c2, s2], axis=1) * l ** -0.5).astype(BF16)
    const = lambda a: pl.BlockSpec(a.shape, lambda bi: (0,) * a.ndim, pipeline_mode=pl.Buffered(1))
    out = pl.pallas_call(
        functools.partial(_fourier_kernel, fg=fg, n1=n1),
        out_shape=jax.ShapeDtypeStruct((b, n2, n1 * df), BF16),
        grid=(b,),
        in_specs=[pl.BlockSpec((1, l, df), lambda bi: (bi, 0, 0)), const(wc), const(tw), const(cs)],
        out_specs=pl.BlockSpec((1, n2, n1 * df), lambda bi: (bi, 0, 0)),
        scratch_shapes=[pltpu.VMEM((n1, 2 * n2, df), BF16)],
        compiler_params=_params("parallel"),
        name="fourier",
    )(u, wc, tw, cs)
    return out.reshape(b, l, df)


def _merge_kernel(fu_ref, hm_ref, br_ref, x_ref, mod_ref, wf_ref, wm_ref, wo_ref, g2_ref,
                  wr1_ref, wr2_ref, x1_ref, hx_ref, aff_ref):
    tm, d = x_ref.shape[1], x_ref.shape[2]
    ne = aff_ref.shape[2]
    for r0 in range(0, tm, MERGE_SUBTILE):
        rows = slice(r0, r0 + MERGE_SUBTILE)
        yf = jnp.dot(fu_ref[0, rows, :], wf_ref[...], preferred_element_type=F32)
        ym = jnp.dot(hm_ref[0, rows, :], wm_ref[...], preferred_element_type=F32)
        gf = jax.nn.sigmoid(br_ref[0, rows, 0:d].astype(F32))
        gm = jax.nn.sigmoid(br_ref[0, rows, d:2 * d].astype(F32))
        merged = (gf * yf + gm * ym).astype(BF16)
        o = jnp.dot(merged, wo_ref[...], preferred_element_type=F32)
        x1 = x_ref[0, rows, :] + mod_ref[0, 2:3, :] * o
        x1_ref[0, rows, :] = x1
        h2 = _rms_mod(x1, g2_ref[...], mod_ref[0, 3:4, :], mod_ref[0, 4:5, :])
        hi = h2.astype(BF16)
        hx_ref[0, rows, :] = hi
        lo = (h2 - hi.astype(F32)).astype(BF16)
        r = (jnp.dot(hi, wr1_ref[...], preferred_element_type=F32)
             + jnp.dot(lo, wr2_ref[...], preferred_element_type=F32))
        logits = r[:, 0:ne] + r[:, ne:2 * ne]
        e = jnp.exp(logits - jnp.max(logits, axis=-1, keepdims=True))
        aff_ref[0, rows, :] = e / jnp.sum(e, axis=-1, keepdims=True)


def _merge(fu, hm, pbr, x, mods, wf, wm, wo, g2, w_router, tm):
    b, l, d = x.shape
    ne = w_router.shape[1]
    whi = w_router.astype(BF16)
    wlo = (w_router - whi.astype(F32)).astype(BF16)
    pad = jnp.zeros((d, LANES - 2 * ne), BF16)
    wr1 = jnp.concatenate([whi, wlo, pad], axis=1)
    wr2 = jnp.concatenate([whi, jnp.zeros((d, ne), BF16), pad], axis=1)
    tok = lambda n: pl.BlockSpec((1, tm, n), lambda bi, i: (bi, i, 0))
    const = lambda a: pl.BlockSpec(a.shape, lambda bi, i: (0, 0), pipeline_mode=pl.Buffered(1))
    return pl.pallas_call(
        _merge_kernel,
        out_shape=[jax.ShapeDtypeStruct((b, l, d), F32),
                   jax.ShapeDtypeStruct((b, l, d), BF16),
                   jax.ShapeDtypeStruct((b, l, ne), F32)],
        grid=(b, l // tm),
        in_specs=[tok(fu.shape[2]), tok(hm.shape[2]), tok(pbr.shape[2]), tok(d),
                  pl.BlockSpec((1, N_MOD, d), lambda bi, i: (bi, 0, 0)),
                  const(wf), const(wm), const(wo),
                  pl.BlockSpec((1, d), lambda bi, i: (0, 0)),
                  const(wr1), const(wr2)],
        out_specs=[tok(d), tok(d), tok(ne)],
        compiler_params=_params("parallel", "parallel"),
        name="merge",
    )(fu, hm, pbr, x, mods, wf, wm, wo, g2.reshape(1, d), wr1, wr2)


def _topk_kernel(a_ref, u_ref, slot_ref, *, k):
    a = a_ref[...]
    l = a.shape[1]
    total = lambda m: jnp.sum(m, axis=1, keepdims=True)
    ge = lambda bits: jnp.where(a >= pltpu.bitcast(bits, F32), 1.0, 0.0)

    def body(i, t):
        cand = t | lax.shift_left(jnp.int32(1), 30 - i)
        return jnp.where(total(ge(cand)) >= k, cand, t)

    t = lax.fori_loop(0, 31, body, jnp.zeros((a.shape[0], 1), jnp.int32))
    chosen = ge(t + 1)
    rem = ge(t) - chosen
    need = k - total(chosen)
    idx = lax.broadcasted_iota(jnp.int32, a.shape, 1).astype(F32)
    for _ in range(N_TIE_EXTRACT):
        cur = jnp.where(rem > 0.0, a, -1.0)
        is_mx = jnp.where(cur == jnp.max(cur, axis=1, keepdims=True), rem, 0.0)
        first = jnp.min(jnp.where(is_mx > 0.0, idx, float(l)), axis=1, keepdims=True)
        pick = jnp.where(idx == first, jnp.where(need > 0.0, 1.0, 0.0), 0.0)
        chosen = chosen + pick
        rem = rem - pick
        need = need - total(pick)
    before = jnp.dot(rem.astype(BF16), u_ref[...], preferred_element_type=F32)
    sel = chosen + jnp.where(before < need, rem, 0.0)
    rank = jnp.dot(sel.astype(BF16), u_ref[...], preferred_element_type=F32)
    slot_ref[...] = jnp.where(sel > 0.0, rank, -1.0)


def _topk(aff_t, k):
    r, l = aff_t.shape
    tr = min(128, r)
    idx = jnp.arange(l, dtype=jnp.int32)
    upper = (idx[:, None] < idx[None, :]).astype(BF16)
    return pl.pallas_call(
        functools.partial(_topk_kernel, k=k),
        out_shape=jax.ShapeDtypeStruct((r, l), F32),
        grid=(r // tr,),
        in_specs=[pl.BlockSpec((tr, l), lambda i: (i, 0)),
                  pl.BlockSpec((l, l), lambda i: (0, 0), pipeline_mode=pl.Buffered(1))],
        out_specs=pl.BlockSpec((tr, l), lambda i: (i, 0)),
        compiler_params=_params("parallel"),
        name="topk",
    )(aff_t, upper)


def _moe_kernel(hx_ref, srow_ref, arow_ref, scol_ref, wg_ref, wu_ref, wd_ref, x1_ref, mod_ref,
                g_ref, o_ref, y_scr, *, cap, ne):
    s = pl.program_id(1)

    @pl.when(s < ne)
    def _():
        l = hx_ref.shape[1]
        slot_i = lax.broadcasted_iota(jnp.int32, (cap, l), 0).astype(F32)
        hit = srow_ref[0, 0] == slot_i
        xs = jnp.dot(jnp.where(hit, 1.0, 0.0).astype(BF16), hx_ref[0],
                     preferred_element_type=F32).astype(BF16)
        gate = jnp.dot(xs, wg_ref[0], preferred_element_type=F32)
        up = jnp.dot(xs, wu_ref[0], preferred_element_type=F32)
        hid = (_silu(gate) * up).astype(BF16)
        y = jnp.dot(hid, wd_ref[0], preferred_element_type=F32)
        vals = jnp.sum(jnp.where(hit, arow_ref[0, 0], 0.0), axis=1, keepdims=True)
        y_scr[pl.ds(pl.multiple_of(s * cap, cap), cap), :] = (y * vals).astype(BF16)

    @pl.when(s >= ne)
    def _():
        tt = x1_ref.shape[1]
        scol = scol_ref[0]
        slot_j = lax.broadcasted_iota(jnp.int32, (tt, cap), 1).astype(F32)
        pt = jnp.concatenate(
            [jnp.where(jnp.broadcast_to(scol[:, e:e + 1], (tt, cap)) == slot_j, 1.0, 0.0).astype(BF16)
             for e in range(ne)], axis=1)
        moe = jnp.dot(pt, y_scr[...], preferred_element_type=F32)
        v = x1_ref[0] + mod_ref[0, 5:6, :] * moe
        ms = jnp.mean(v * v, axis=-1, keepdims=True)
        o_ref[0] = v * lax.rsqrt(ms + EPS) * g_ref[...]


def _moe(hx, slot_row, aff_row, slot_col, wg, wu, wd, x1, mods, g_final, cap):
    b, l, d = hx.shape
    ne, _, f = wg.shape
    tt = 256
    ex = lambda s: jnp.minimum(s, ne - 1)
    tile = lambda s: jnp.maximum(s - ne, 0)
    row = pl.BlockSpec((1, 1, 1, l), lambda bi, s: (bi, ex(s), 0, 0))
    wspec = lambda k, n: pl.BlockSpec((1, k, n), lambda bi, s: (ex(s), 0, 0))
    return pl.pallas_call(
        functools.partial(_moe_kernel, cap=cap, ne=ne),
        out_shape=jax.ShapeDtypeStruct((b, l, d), F32),
        grid=(b, ne + l // tt),
        in_specs=[pl.BlockSpec((1, l, d), lambda bi, s: (bi, 0, 0)),
                  row, row,
                  pl.BlockSpec((1, tt, ne), lambda bi, s: (bi, tile(s), 0)),
                  wspec(d, f), wspec(d, f), wspec(f, d),
                  pl.BlockSpec((1, tt, d), lambda bi, s: (bi, tile(s), 0)),
                  pl.BlockSpec((1, N_MOD, d), lambda bi, s: (bi, 0, 0)),
                  pl.BlockSpec((1, d), lambda bi, s: (0, 0))],
        out_specs=pl.BlockSpec((1, tt, d), lambda bi, s: (bi, tile(s), 0)),
        scratch_shapes=[pltpu.VMEM((ne * cap, d), BF16)],
        compiler_params=_params("parallel", "arbitrary"),
        name="moe",
    )(hx, slot_row, aff_row, slot_col, wg, wu, wd, x1, mods, g_final.reshape(1, d))


def _scan_tables(g_t, b_gates, chunk):
    rr = _gateprep(g_t, b_gates, chunk)
    return rr, jnp.swapaxes(rr, 2, 3)


def kernel(x, c, ctx, c_ctx, w_ada, b_ada, g_norm1, w_in, b_gates, w_conv, w_fourier, w_mlstm,
           w_out, g_norm2, w_router, w_gate_e, w_up_e, w_down_e, g_final):
    assert w_ada.shape[0] == 1, "single-layer stack"
    b, l, d = x.shape
    lc = ctx.shape[1]
    d_f = d // 2
    d_m = d // 2
    dh = d_m // N_H
    ne = w_router.shape[2]
    cap = CAP_FACTOR * l // ne
    off_qk = d_f
    off_v = off_qk + 2 * d_m
    off_o = off_v + d_m
    off_g = off_o + d_m
    off_br = off_g + N_GATE_KINDS * N_H

    n_c = b + 1
    n_pad = -n_c % 8
    c_all = jnp.concatenate([c, c_ctx[None, :], jnp.zeros((n_pad, d), F32)], axis=0)
    mods = _ada(c_all, w_ada[0], b_ada[0]).reshape(n_c + n_pad, N_MOD, d)

    w = w_in[0]
    wb = w.astype(BF16)
    w_f, w_qk, w_v, w_o, w_br = (wb[:, 0:off_qk], wb[:, off_qk:off_v], wb[:, off_v:off_o],
                                 wb[:, off_o:off_g], wb[:, off_br:])
    w_gt = wb[:, off_g:off_br].T
    w9 = w_conv[0].reshape(9, 2 * d_m)
    qk_scale = jnp.concatenate([jnp.ones((1, d_m), F32), jnp.full((1, d_m), dh ** -0.5, F32)], axis=1)

    pc_qk, pc_v, gc_t = _inproj(ctx, mods, lambda bi, i: (b, 0, 0), g_norm1[0],
                                [w_qk, w_v], w_gt, tm=lc)
    q_c = _conv(pc_qk, w9, qk_scale, 1, 0, d_m, False)
    kt_c = _conv(pc_qk, w9, qk_scale, 1, d_m, 2 * d_m, True)
    rr_c, rc_c = _scan_tables(gc_t, b_gates[0], min(SCAN_CHUNK, lc))
    c0 = jnp.zeros((b, 2, N_H, dh, 2 * dh), F32)
    m0 = jnp.zeros((b, 2, N_H, 1, LANES), F32)
    c_seed, m_seed = _scan(q_c, kt_c, pc_v, None, rr_c, rc_c, c0, m0, want_h=False, want_state=True)

    p_f, p_qk, p_v, p_o, p_br, g_t = _inproj(x, mods, lambda bi, i: (bi, 0, 0), g_norm1[0],
                                             [w_f, w_qk, w_v, w_o, w_br], w_gt, tm=512)
    q_x = _conv(p_qk, w9, qk_scale, l // GRID_W, 0, d_m, False)
    kt_x = _conv(p_qk, w9, qk_scale, l // GRID_W, d_m, 2 * d_m, True)
    rr_x, rc_x = _scan_tables(g_t, b_gates[0], SCAN_CHUNK)
    (hm,) = _scan(q_x, kt_x, p_v, p_o, rr_x, rc_x, c_seed, m_seed, want_h=True, want_state=False)
    fu = _fourier(p_f)
    x1, hx2, aff = _merge(fu, hm, p_br, x, mods, w_fourier[0].astype(BF16), w_mlstm[0].astype(BF16),
                          w_out[0].astype(BF16), g_norm2[0], w_router[0], tm=512)

    aff_t = jnp.swapaxes(aff, 1, 2)
    slot_t = _topk(aff_t.reshape(b * ne, l), cap).reshape(b, ne, l)
    return _moe(hx2, slot_t.reshape(b, ne, 1, l), aff_t.reshape(b, ne, 1, l),
                jnp.swapaxes(slot_t, 1, 2), w_gate_e[0].astype(BF16), w_up_e[0].astype(BF16),
                w_down_e[0].astype(BF16), x1, mods, g_final, cap)
```

```python
import functools
import math

import jax
import jax.numpy as jnp
from jax import lax
from jax.experimental import pallas as pl
from jax.experimental.pallas import tpu as pltpu

F32 = jnp.float32
BF16 = jnp.bfloat16

GRID_W = 64
N_FG = 4
N_H = 4
N_GATE_KINDS = 4
N_EXPERTS = 16
CAP_FACTOR = 2
EPS = 1e-6
LOG2E = math.log2(math.e)
N_MOD = 6

LANES = 128
V7X_VMEM_BYTES = 64 * 1024 * 1024
VMEM_LIMIT = V7X_VMEM_BYTES - 8 * 1024 * 1024

SCAN_CHUNK = 256
N_SCAN_ROWS = 8
N_TIE_EXTRACT = 3
MERGE_SUBTILE = 256
MOE_TILE = 256
MOE_WINDOW = 64
MOE_GATHER_WINDOW = 128
ROW_ALIGN = 8
SLOT_ALIGN = 16


def _params(*sem):
    return pltpu.CompilerParams(dimension_semantics=sem, vmem_limit_bytes=VMEM_LIMIT)


def _silu(v):
    return v * jax.nn.sigmoid(v)


def _rms_mod(v, g, shift, scale):
    ms = jnp.mean(v * v, axis=-1, keepdims=True)
    return (v * lax.rsqrt(ms + EPS) * g) * (1.0 + scale) + shift


def _ada_kernel(c_ref, w_ref, b_ref, o_ref):
    s = _silu(c_ref[...])
    o_ref[...] = jnp.dot(s, w_ref[...], preferred_element_type=F32,
                         precision=lax.Precision.HIGHEST) + b_ref[...]


def _ada(c_all, w_ada, b_ada):
    n, d = c_all.shape
    cols = w_ada.shape[1]
    tn = 512
    return pl.pallas_call(
        _ada_kernel,
        out_shape=jax.ShapeDtypeStruct((n, cols), F32),
        grid=(cols // tn,),
        in_specs=[pl.BlockSpec((n, d), lambda j: (0, 0)),
                  pl.BlockSpec((d, tn), lambda j: (0, j)),
                  pl.BlockSpec((1, tn), lambda j: (0, j))],
        out_specs=pl.BlockSpec((n, tn), lambda j: (0, j)),
        compiler_params=_params("parallel"),
        name="ada",
    )(c_all, w_ada, b_ada.reshape(1, cols))


def _inproj_kernel(x_ref, mod_ref, g_ref, *refs, n_seg):
    w_refs = refs[:n_seg]
    wgt_ref = refs[n_seg]
    o_refs = refs[n_seg + 1:2 * n_seg + 1]
    gt_ref = refs[2 * n_seg + 1]
    h = _rms_mod(x_ref[0], g_ref[...], mod_ref[0, 0:1, :], mod_ref[0, 1:2, :])
    hb = h.astype(BF16)
    for w_ref, o_ref in zip(w_refs, o_refs):
        o_ref[0] = jnp.dot(hb, w_ref[...], preferred_element_type=F32).astype(o_ref.dtype)
    gt_ref[0] = lax.dot_general(wgt_ref[...], hb, (((1,), (1,)), ((), ())),
                                preferred_element_type=F32)


def _inproj(x, mods, mod_row, g, w_segs, w_gt, tm):
    b, l, d = x.shape
    n_seg = len(w_segs)
    ng = w_gt.shape[0]
    const2 = lambda bi, i: (0, 0)
    in_specs = [pl.BlockSpec((1, tm, d), lambda bi, i: (bi, i, 0)),
                pl.BlockSpec((1, N_MOD, d), mod_row),
                pl.BlockSpec((1, d), const2)]
    in_specs += [pl.BlockSpec(w.shape, const2, pipeline_mode=pl.Buffered(1)) for w in w_segs]
    in_specs += [pl.BlockSpec(w_gt.shape, const2)]
    out_shape = [jax.ShapeDtypeStruct((b, l, w.shape[1]), BF16) for w in w_segs]
    out_shape += [jax.ShapeDtypeStruct((b, ng, l), F32)]
    out_specs = [pl.BlockSpec((1, tm, w.shape[1]), lambda bi, i: (bi, i, 0)) for w in w_segs]
    out_specs += [pl.BlockSpec((1, ng, tm), lambda bi, i: (bi, 0, i))]
    return pl.pallas_call(
        functools.partial(_inproj_kernel, n_seg=n_seg),
        out_shape=out_shape,
        grid=(b, l // tm),
        in_specs=in_specs,
        out_specs=out_specs,
        compiler_params=_params("parallel", "parallel"),
        name="inproj",
    )(x, mods, g.reshape(1, d), *w_segs, w_gt)


def _conv_kernel(p_ref, w_ref, s_ref, o_ref, *, rows, width, transpose):
    v = p_ref[0].astype(F32)
    l = v.shape[0]
    pos = lax.broadcasted_iota(jnp.int32, v.shape, 0)
    col = pos % width
    vl = jnp.where(col == 0, 0.0, pltpu.roll(v, 1, 0))
    vr = jnp.where(col == width - 1, 0.0, pltpu.roll(v, l - 1, 0))
    w = w_ref[...]

    def tap(dr):
        return (vl * w[3 * dr:3 * dr + 1, :] + v * w[3 * dr + 1:3 * dr + 2, :]
                + vr * w[3 * dr + 2:3 * dr + 3, :])

    acc = tap(1)
    if rows > 1:
        acc = acc + jnp.where(pos < width, 0.0, pltpu.roll(tap(0), width, 0))
        acc = acc + jnp.where(pos >= l - width, 0.0, pltpu.roll(tap(2), l - width, 0))
    y = _silu(acc) * s_ref[...]
    o_ref[0] = (y.T if transpose else y).astype(o_ref.dtype)


def _conv(p, w9, scale, rows, col_lo, col_hi, transpose):
    b, l, _ = p.shape
    ct = 256
    j0 = col_lo // ct
    n = col_hi - col_lo
    if transpose:
        out_shape = jax.ShapeDtypeStruct((b, n, l), BF16)
        out_spec = pl.BlockSpec((1, ct, l), lambda bi, j: (bi, j, 0))
    else:
        out_shape = jax.ShapeDtypeStruct((b, l, n), BF16)
        out_spec = pl.BlockSpec((1, l, ct), lambda bi, j: (bi, 0, j))
    return pl.pallas_call(
        functools.partial(_conv_kernel, rows=rows, width=l // rows, transpose=transpose),
        out_shape=out_shape,
        grid=(b, n // ct),
        in_specs=[pl.BlockSpec((1, l, ct), lambda bi, j: (bi, 0, j + j0)),
                  pl.BlockSpec((9, ct), lambda bi, j: (0, j + j0)),
                  pl.BlockSpec((1, ct), lambda bi, j: (0, j + j0))],
        out_specs=out_spec,
        compiler_params=_params("parallel", "parallel"),
        name="conv_kt" if transpose else "conv_q",
    )(p, w9, scale)


def _log_sigmoid(v):
    return jnp.minimum(v, 0.0) - jnp.log1p(jnp.exp(-jnp.abs(v)))


def _chunk_scan(v, op, ident, chunk, reverse):
    l = v.shape[1]
    pos = lax.broadcasted_iota(jnp.int32, v.shape, 1) % chunk
    s = 1
    while s < chunk:
        if reverse:
            shifted = jnp.where(pos < chunk - s, pltpu.roll(v, l - s, 1), ident)
        else:
            shifted = jnp.where(pos >= s, pltpu.roll(v, s, 1), ident)
        v = op(v, shifted)
        s *= 2
    return v


def _gateprep_kernel(g_ref, b_ref, o_ref, *, chunk):
    g = g_ref[0] + b_ref[...]
    i_f = g[0:N_H] * LOG2E
    f_f = _log_sigmoid(g[N_H:2 * N_H]) * LOG2E
    i_b = g[2 * N_H:3 * N_H] * LOG2E
    f_b = _log_sigmoid(g[3 * N_H:4 * N_H]) * LOG2E
    neg = -jnp.inf
    cum_f = _chunk_scan(f_f, jnp.add, 0.0, chunk, False)
    br_f = i_f - cum_f
    pm_f = _chunk_scan(br_f, jnp.maximum, neg, chunk, False)
    cum_b = _chunk_scan(f_b, jnp.add, 0.0, chunk, True)
    br_b = i_b - cum_b
    pm_b = _chunk_scan(br_b, jnp.maximum, neg, chunk, True)
    zero = jnp.zeros_like(cum_f[0:1])
    for h in range(N_H):
        o_ref[0, h] = jnp.concatenate(
            [a[h:h + 1] for a in (cum_f, br_f, pm_f, cum_b, br_b, pm_b)] + [zero, zero], axis=0)


def _gateprep(g_t, b_gates, chunk):
    b, ng, l = g_t.shape
    return pl.pallas_call(
        functools.partial(_gateprep_kernel, chunk=chunk),
        out_shape=jax.ShapeDtypeStruct((b, N_H, N_SCAN_ROWS, l), F32),
        grid=(b,),
        in_specs=[pl.BlockSpec((1, ng, l), lambda bi: (bi, 0, 0)),
                  pl.BlockSpec((ng, 1), lambda bi: (0, 0))],
        out_specs=pl.BlockSpec((1, N_H, N_SCAN_ROWS, l), lambda bi: (bi, 0, 0, 0)),
        compiler_params=_params("parallel"),
        name="gateprep",
    )(g_t, b_gates.reshape(ng, 1))


def _scan_kernel(*refs, chunk, want_h, want_state):
    q_ref, kt_ref, v_ref = refs[:3]
    pos = 3
    if want_h:
        og_ref = refs[pos]
        pos += 1
    rr_ref, c0_ref, m0_ref = refs[pos:pos + 3]
    pos += 3
    if want_h:
        h_ref = refs[pos]
        pos += 1
    if want_state:
        cs_ref, ms_ref = refs[pos:pos + 2]
        pos += 2
    vext_scr = refs[pos]
    if want_h:
        hf_scr, hb_scr = refs[pos + 1:pos + 3]

    l, dh = q_ref.shape[1], q_ref.shape[2]
    nc = l // chunk
    rep = chunk // dh
    vext_scr[:, 0:dh] = v_ref[0]
    vext_scr[:, dh:2 * dh] = jnp.ones((l, dh), BF16)

    jj = lax.broadcasted_iota(jnp.int32, (chunk, chunk), 0)
    ss = lax.broadcasted_iota(jnp.int32, (chunk, chunk), 1)
    causal = (jj >= ss, jj <= ss)

    cols = rr_ref[0, 0].T
    cst = [c0_ref[0, d, 0] for d in range(2)]
    mst = [m0_ref[0, d, 0][:, 0:1] for d in range(2)]

    for i in range(nc):
        for d in range(2):
            c = i if d == 0 else nc - 1 - i
            lo = c * chunk
            end = chunk - 1 if d == 0 else 0
            r0 = 3 * d
            qc = q_ref[0, lo:lo + chunk, :]
            ktc = kt_ref[0, :, lo:lo + chunk]
            vc = vext_scr[lo:lo + chunk, :]
            cum_c = cols[lo:lo + chunk, r0:r0 + 1]
            pm_c = cols[lo:lo + chunk, r0 + 2:r0 + 3]
            br_r = rr_ref[0, 0, r0 + 1:r0 + 2, lo:lo + chunk]
            tot = rr_ref[0, 0, r0:r0 + 1, lo + end:lo + end + 1]
            pm_end = rr_ref[0, 0, r0 + 2:r0 + 3, lo + end:lo + end + 1]
            m0 = mst[d]
            mm = jnp.broadcast_to(jnp.maximum(m0, pm_c), (chunk, dh))
            d_in = jnp.where(causal[d], jnp.exp2(br_r - jnp.concatenate([mm] * rep, axis=1)), 0.0)
            d_st = jnp.exp2(m0 - mm)
            s = jnp.dot(qc, ktc, preferred_element_type=F32)
            p = (s * d_in).astype(BF16)
            qd = (qc.astype(F32) * d_st).astype(BF16)
            both = (jnp.dot(p, vc, preferred_element_type=F32)
                    + jnp.dot(qd, cst[d].astype(BF16), preferred_element_type=F32))
            if want_h:
                m_j = jnp.broadcast_to(cum_c, (chunk, dh)) + mm
                hc = both[:, 0:dh] / jnp.maximum(jnp.abs(both[:, dh:2 * dh]), jnp.exp2(-m_j))
                (hf_scr if d == 0 else hb_scr)[lo:lo + chunk, :] = hc
            kw = (ktc.astype(F32) * d_in[end:end + 1, :]).astype(BF16)
            upd = jnp.dot(kw, vc, preferred_element_type=F32)
            dn = d_st[end:end + 1, :]
            cst[d] = jnp.concatenate([dn, dn], axis=1) * cst[d] + upd
            mst[d] = tot + jnp.maximum(m0, pm_end)

    if want_h:
        gate = jax.nn.sigmoid(og_ref[0].astype(F32))
        h_ref[0] = (gate * (hf_scr[...] + hb_scr[...])).astype(h_ref.dtype)
    if want_state:
        for d in range(2):
            cs_ref[0, d, 0] = cst[d]
            ms_ref[0, d, 0] = jnp.broadcast_to(mst[d], (1, LANES))


def _scan(q, kt, v, og, rr, c0, m0, *, want_h, want_state):
    b, l, dm = v.shape
    dh = dm // N_H
    chunk = min(SCAN_CHUNK, l)
    assert chunk % dh == 0 and l % chunk == 0
    tok = pl.BlockSpec((1, l, dh), lambda bi, h: (bi, 0, h))
    in_specs = [tok, pl.BlockSpec((1, dh, l), lambda bi, h: (bi, h, 0)), tok]
    args = [q, kt, v]
    if want_h:
        in_specs.append(tok)
        args.append(og)
    st_c = pl.BlockSpec((1, 2, 1, dh, 2 * dh), lambda bi, h: (bi, 0, h, 0, 0))
    st_m = pl.BlockSpec((1, 2, 1, 1, LANES), lambda bi, h: (bi, 0, h, 0, 0))
    in_specs += [pl.BlockSpec((1, 1, N_SCAN_ROWS, l), lambda bi, h: (bi, h, 0, 0)), st_c, st_m]
    args += [rr, c0, m0]
    out_shape, out_specs = [], []
    scratch = [pltpu.VMEM((l, 2 * dh), BF16)]
    if want_h:
        out_shape.append(jax.ShapeDtypeStruct((b, l, dm), BF16))
        out_specs.append(tok)
        scratch += [pltpu.VMEM((l, dh), F32), pltpu.VMEM((l, dh), F32)]
    if want_state:
        out_shape += [jax.ShapeDtypeStruct(c0.shape, F32), jax.ShapeDtypeStruct(m0.shape, F32)]
        out_specs += [st_c, st_m]
    return pl.pallas_call(
        functools.partial(_scan_kernel, chunk=chunk, want_h=want_h, want_state=want_state),
        out_shape=out_shape,
        grid=(b, N_H),
        in_specs=in_specs,
        out_specs=out_specs,
        scratch_shapes=scratch,
        compiler_params=_params("parallel", "parallel"),
        name="scan_h" if want_h else "scan_state",
    )(*args)


def _twiddle(z, k, n):
    zr, zi = z
    k %= n
    if k == 0:
        return z
    if 4 * k == n:
        return zi, -zr
    if 2 * k == n:
        return -zr, -zi
    if 4 * k == 3 * n:
        return -zi, zr
    wr, wi = math.cos(2.0 * math.pi * k / n), -math.sin(2.0 * math.pi * k / n)
    return zr * wr - zi * wi, zr * wi + zi * wr


def _fft_list(xs):
    n = len(xs)
    if n == 1:
        return xs
    ev, od = _fft_list(xs[0::2]), _fft_list(xs[1::2])
    out = [None] * n
    for k in range(n // 2):
        tr, ti = _twiddle(od[k], k, n)
        out[k] = (ev[k][0] + tr, ev[k][1] + ti)
        out[k + n // 2] = (ev[k][0] - tr, ev[k][1] - ti)
    return out


def _fourier_kernel(u_ref, wc_ref, tw_ref, cs_ref, o_ref, zz_scr, r_scr, *, fg, n1):
    l, df = u_ref.shape[1], u_ref.shape[2]
    n2 = l // n1
    for g in range(df // fg):
        r = jnp.dot(u_ref[0, :, g * fg:(g + 1) * fg], wc_ref[...], preferred_element_type=F32)
        xs = [(r[j * n2:(j + 1) * n2, 0:fg], r[j * n2:(j + 1) * n2, fg:2 * fg]) for j in range(n1)]
        zs = _fft_list(xs)
        for k1 in range(n1):
            zr, zi = zs[k1]
            if k1 > 0:
                tc, ts = tw_ref[0, k1], tw_ref[1, k1]
                zr, zi = zr * tc + zi * ts, zi * tc - zr * ts
            zz_scr[k1, 0:n2, g * fg:(g + 1) * fg] = zr.astype(BF16)
            zz_scr[k1, n2:2 * n2, g * fg:(g + 1) * fg] = zi.astype(BF16)
    pitch = r_scr.shape[1] // n1
    for k1 in range(n1):
        res = jnp.dot(cs_ref[...], zz_scr[k1], preferred_element_type=F32)
        for c in range(df // LANES):
            r_scr[c, k1 * pitch:k1 * pitch + n2, :] = res[:, c * LANES:(c + 1) * LANES]

    def emit(j, carry):
        for c in range(df // LANES):
            a = r_scr[c, pl.ds(2 * j, n1, stride=pitch), :]
            b = r_scr[c, pl.ds(2 * j + 1, n1, stride=pitch), :]
            o_ref[0, pl.ds(pl.multiple_of(2 * n1 * j, 2 * n1), 2 * n1), c * LANES:(c + 1) * LANES] = (
                jnp.concatenate([a, b], axis=0).astype(o_ref.dtype))
        return carry

    lax.fori_loop(0, n2 // 2, emit, 0, unroll=8)


def _dft_angles(rows, cols, n):
    ang = ((rows[:, None] * cols[None, :]) % n).astype(F32) * (2.0 * math.pi / n)
    return jnp.cos(ang), jnp.sin(ang)


def _fourier(u):
    b, l, df = u.shape
    fg = df // N_FG
    n1 = 8
    n2 = l // n1
    ar = lambda n: jnp.arange(n, dtype=jnp.int32)
    cc, sc = _dft_angles(ar(fg), ar(fg), fg)
    wc = (jnp.concatenate([cc, -sc], axis=1) * fg ** -0.5).astype(BF16)
    tc, ts = _dft_angles(ar(n1), ar(n2), l)
    tw = jnp.broadcast_to(jnp.stack([tc, ts])[..., None], (2, n1, n2, fg))
    c2, s2 = _dft_angles(ar(n2), ar(n2), n2)
    cs = (jnp.concatenate([c2, s2], axis=1) * l ** -0.5).astype(BF16)
    const = lambda a: pl.BlockSpec(a.shape, lambda bi: (0,) * a.ndim, pipeline_mode=pl.Buffered(1))
    pitch = n2 + 4
    return pl.pallas_call(
        functools.partial(_fourier_kernel, fg=fg, n1=n1),
        out_shape=jax.ShapeDtypeStruct((b, l, df), BF16),
        grid=(b,),
        in_specs=[pl.BlockSpec((1, l, df), lambda bi: (bi, 0, 0)), const(wc), const(tw), const(cs)],
        out_specs=pl.BlockSpec((1, l, df), lambda bi: (bi, 0, 0)),
        scratch_shapes=[pltpu.VMEM((n1, 2 * n2, df), BF16),
                        pltpu.VMEM((df // LANES, n1 * pitch, LANES), F32)],
        compiler_params=_params("parallel"),
        name="fourier",
    )(u, wc, tw, cs)


def _merge_kernel(fu_ref, hm_ref, br_ref, x_ref, mod_ref, wf_ref, wm_ref, wo_ref, g2_ref,
                  wr1_ref, wr2_ref, x1_ref, hx_ref, aff_ref):
    tm, d = x_ref.shape[1], x_ref.shape[2]
    ne = aff_ref.shape[1]
    for r0 in range(0, tm, MERGE_SUBTILE):
        rows = slice(r0, r0 + MERGE_SUBTILE)
        yf = jnp.dot(fu_ref[0, rows, :], wf_ref[...], preferred_element_type=F32)
        ym = jnp.dot(hm_ref[0, rows, :], wm_ref[...], preferred_element_type=F32)
        gf = jax.nn.sigmoid(br_ref[0, rows, 0:d].astype(F32))
        gm = jax.nn.sigmoid(br_ref[0, rows, d:2 * d].astype(F32))
        merged = (gf * yf + gm * ym).astype(BF16)
        o = jnp.dot(merged, wo_ref[...], preferred_element_type=F32)
        x1 = x_ref[0, rows, :] + mod_ref[0, 2:3, :] * o
        x1_ref[0, rows, :] = x1
        h2 = _rms_mod(x1, g2_ref[...], mod_ref[0, 3:4, :], mod_ref[0, 4:5, :])
        hi = h2.astype(BF16)
        hx_ref[0, rows, :] = hi
        lo = (h2 - hi.astype(F32)).astype(BF16)
        r = (jnp.dot(hi, wr1_ref[...], preferred_element_type=F32)
             + jnp.dot(lo, wr2_ref[...], preferred_element_type=F32))
        rt = r.T
        logits = rt[0:ne, :] + rt[ne:2 * ne, :]
        e = jnp.exp(logits - jnp.max(logits, axis=0, keepdims=True))
        aff_ref[0, :, rows] = e / jnp.sum(e, axis=0, keepdims=True)


def _merge(fu, hm, pbr, x, mods, wf, wm, wo, g2, w_router, tm):
    b, l, d = x.shape
    ne = w_router.shape[1]
    whi = w_router.astype(BF16)
    wlo = (w_router - whi.astype(F32)).astype(BF16)
    pad = jnp.zeros((d, LANES - 2 * ne), BF16)
    wr1 = jnp.concatenate([whi, wlo, pad], axis=1)
    wr2 = jnp.concatenate([whi, jnp.zeros((d, ne), BF16), pad], axis=1)
    tok = lambda n: pl.BlockSpec((1, tm, n), lambda bi, i: (bi, i, 0))
    const = lambda a: pl.BlockSpec(a.shape, lambda bi, i: (0, 0), pipeline_mode=pl.Buffered(1))
    return pl.pallas_call(
        _merge_kernel,
        out_shape=[jax.ShapeDtypeStruct((b, l, d), F32),
                   jax.ShapeDtypeStruct((b, l, d), BF16),
                   jax.ShapeDtypeStruct((b, ne, l), F32)],
        grid=(b, l // tm),
        in_specs=[tok(fu.shape[2]), tok(hm.shape[2]), tok(pbr.shape[2]), tok(d),
                  pl.BlockSpec((1, N_MOD, d), lambda bi, i: (bi, 0, 0)),
                  const(wf), const(wm), const(wo),
                  pl.BlockSpec((1, d), lambda bi, i: (0, 0)),
                  const(wr1), const(wr2)],
        out_specs=[tok(d), tok(d), pl.BlockSpec((1, ne, tm), lambda bi, i: (bi, 0, i))],
        compiler_params=_params("parallel", "parallel"),
        name="merge",
    )(fu, hm, pbr, x, mods, wf, wm, wo, g2.reshape(1, d), wr1, wr2)


def _topk_kernel(a_ref, u_ref, slot_ref, before_ref, *, k):
    a = a_ref[...]
    l = a.shape[1]
    total = lambda m: jnp.sum(m, axis=1, keepdims=True)
    ge = lambda bits: jnp.where(a >= pltpu.bitcast(bits, F32), 1.0, 0.0)

    def body(i, t):
        cand = t | lax.shift_left(jnp.int32(1), 30 - i)
        return jnp.where(total(ge(cand)) >= k, cand, t)

    t = lax.fori_loop(0, 31, body, jnp.zeros((a.shape[0], 1), jnp.int32))
    chosen = ge(t + 1)
    rem = ge(t) - chosen
    need = k - total(chosen)
    idx = lax.broadcasted_iota(jnp.int32, a.shape, 1).astype(F32)
    for _ in range(N_TIE_EXTRACT):
        cur = jnp.where(rem > 0.0, a, -1.0)
        is_mx = jnp.where(cur == jnp.max(cur, axis=1, keepdims=True), rem, 0.0)
        first = jnp.min(jnp.where(is_mx > 0.0, idx, float(l)), axis=1, keepdims=True)
        pick = jnp.where(idx == first, jnp.where(need > 0.0, 1.0, 0.0), 0.0)
        chosen = chosen + pick
        rem = rem - pick
        need = need - total(pick)
    before = jnp.dot(rem.astype(BF16), u_ref[...], preferred_element_type=F32)
    sel = chosen + jnp.where(before < need, rem, 0.0)
    rank = jnp.dot(sel.astype(BF16), u_ref[...], preferred_element_type=F32)
    slot_ref[...] = jnp.where(sel > 0.0, rank, -1.0)
    before_ref[...] = rank


def _topk(aff_t, k):
    r, l = aff_t.shape
    tr = min(128, r)
    idx = jnp.arange(l, dtype=jnp.int32)
    upper = (idx[:, None] < idx[None, :]).astype(BF16)
    return pl.pallas_call(
        functools.partial(_topk_kernel, k=k),
        out_shape=[jax.ShapeDtypeStruct((r, l), F32)] * 2,
        grid=(r // tr,),
        in_specs=[pl.BlockSpec((tr, l), lambda i: (i, 0)),
                  pl.BlockSpec((l, l), lambda i: (0, 0), pipeline_mode=pl.Buffered(1))],
        out_specs=[pl.BlockSpec((tr, l), lambda i: (i, 0))] * 2,
        compiler_params=_params("parallel"),
        name="topk",
    )(aff_t, upper)


def _moe_kernel(cnt_ref, hx_ref, srow_ref, arow_ref, slot_ref, wg_ref, wu_ref, wd_ref, x1_ref,
                mod_ref, g_ref, xp_ref, o_ref, y_scr, xs_scr, *, cap, ne, nt):
    bi = pl.program_id(0)
    s = pl.program_id(1)
    win = MOE_WINDOW

    @pl.when(s == 0)
    def _():
        y_scr[ne * cap:ne * cap + win, :] = jnp.zeros((win, y_scr.shape[1]), BF16)

    @pl.when(s < ne)
    def _():
        l = hx_ref.shape[1]
        tb, gw = MOE_TILE, MOE_GATHER_WINDOW
        xs_scr[...] = jnp.zeros(xs_scr.shape, F32)
        base = bi * (ne * (nt + 1)) + s * (nt + 1)
        row_i = lax.broadcasted_iota(jnp.int32, (gw, tb), 0)
        lows = [(cnt_ref[base + j] // ROW_ALIGN) * ROW_ALIGN for j in range(nt)]
        n_pass = 0
        for j in range(nt):
            n_pass = jnp.maximum(n_pass, (cnt_ref[base + j + 1] - lows[j] + gw - 1) // gw)

        def gather_pass(w, carry):
            for j in range(nt):
                lo = pl.multiple_of(jnp.minimum(lows[j] + w * gw, cap), ROW_ALIGN)
                srow_blk = srow_ref[0, 0, :, j * tb:(j + 1) * tb]
                p = jnp.where(srow_blk == (row_i + lo).astype(F32), 1.0, 0.0).astype(BF16)
                xs_scr[pl.ds(lo, gw), :] += jnp.dot(p, hx_ref[0, j * tb:(j + 1) * tb, :],
                                                    preferred_element_type=F32)
            return carry

        gather_pass(0, 0)
        lax.fori_loop(1, n_pass, gather_pass, 0)
        xs = xs_scr[0:cap, :].astype(BF16)
        slot_i = lax.broadcasted_iota(jnp.int32, (cap, l), 0).astype(F32)
        hit = srow_ref[0, 0] == slot_i
        gate = jnp.dot(xs, wg_ref[0], preferred_element_type=F32)
        up = jnp.dot(xs, wu_ref[0], preferred_element_type=F32)
        hid = (_silu(gate) * up).astype(BF16)
        y = jnp.dot(hid, wd_ref[0], preferred_element_type=F32)
        vals = jnp.sum(jnp.where(hit, arow_ref[0, 0], 0.0), axis=1, keepdims=True)
        y_scr[pl.ds(pl.multiple_of(s * cap, cap), cap), :] = (y * vals).astype(BF16)

    @pl.when(s >= ne)
    def _():
        tt, d = x1_ref.shape[1], x1_ref.shape[2]
        base = bi * (ne * (nt + 1)) + (s - ne)
        lows, n_pass = [], 0
        for e in range(ne):
            start = cnt_ref[base + e * (nt + 1)]
            end = cnt_ref[base + e * (nt + 1) + 1]
            low = (start // SLOT_ALIGN) * SLOT_ALIGN
            lows.append(low)
            n_pass = jnp.maximum(n_pass, (end - low + win - 1) // win)
        sexp = jnp.dot(slot_ref[0].T.astype(BF16), xp_ref[...], preferred_element_type=F32)
        lane = lax.broadcasted_iota(jnp.int32, (1, ne * win), 1)
        grp, off = lane // win, lane % win

        def one_pass(w, acc):
            tgt, rows = off, []
            for e in range(ne):
                lo = jnp.minimum(lows[e] + w * win, cap)
                tgt = jnp.where(grp == e, off + lo, tgt)
                rows.append(y_scr[pl.ds(pl.multiple_of(e * cap + lo, SLOT_ALIGN), win), :])
            pt = jnp.where(sexp == tgt.astype(F32), 1.0, 0.0).astype(BF16)
            return acc + jnp.dot(pt, jnp.concatenate(rows, axis=0), preferred_element_type=F32)

        moe = lax.fori_loop(1, n_pass, one_pass, one_pass(0, jnp.zeros((tt, d), F32)))
        v = x1_ref[0] + mod_ref[0, 5:6, :] * moe
        ms = jnp.mean(v * v, axis=-1, keepdims=True)
        o_ref[0] = v * lax.rsqrt(ms + EPS) * g_ref[...]


def _moe(hx, slot_t, before_t, aff_t, wg, wu, wd, x1, mods, g_final, cap):
    b, l, d = hx.shape
    ne, _, f = wg.shape
    tt = MOE_TILE
    nt = l // tt
    win = MOE_WINDOW
    assert cap % SLOT_ALIGN == 0 and win % SLOT_ALIGN == 0 and (win & (win - 1)) == 0
    slot_row = slot_t.reshape(b, ne, 1, l)
    aff_row = aff_t.reshape(b, ne, 1, l)
    cnt = jnp.concatenate([before_t[:, :, ::tt], jnp.full((b, ne, 1), cap, F32)], axis=2)
    cnt = cnt.astype(jnp.int32).reshape(b * ne * (nt + 1))
    lane = jnp.arange(ne * win, dtype=jnp.int32)
    expand = (lane[None, :] // win == jnp.arange(ne, dtype=jnp.int32)[:, None]).astype(BF16)
    ex = lambda s: jnp.minimum(s, ne - 1)
    tile = lambda s: jnp.maximum(s - ne, 0)
    row = pl.BlockSpec((1, 1, 1, l), lambda bi, s, c: (bi, ex(s), 0, 0))
    wspec = lambda k, n: pl.BlockSpec((1, k, n), lambda bi, s, c: (ex(s), 0, 0))
    grid_spec = pltpu.PrefetchScalarGridSpec(
        num_scalar_prefetch=1,
        grid=(b, ne + nt),
        in_specs=[pl.BlockSpec((1, l, d), lambda bi, s, c: (bi, 0, 0)),
                  row, row,
                  pl.BlockSpec((1, ne, tt), lambda bi, s, c: (bi, 0, tile(s))),
                  wspec(d, f), wspec(d, f), wspec(f, d),
                  pl.BlockSpec((1, tt, d), lambda bi, s, c: (bi, tile(s), 0)),
                  pl.BlockSpec((1, N_MOD, d), lambda bi, s, c: (bi, 0, 0)),
                  pl.BlockSpec((1, d), lambda bi, s, c: (0, 0)),
                  pl.BlockSpec((ne, ne * win), lambda bi, s, c: (0, 0))],
        out_specs=pl.BlockSpec((1, tt, d), lambda bi, s, c: (bi, tile(s), 0)),
        scratch_shapes=[pltpu.VMEM((ne * cap + win, d), BF16),
                        pltpu.VMEM((cap + MOE_GATHER_WINDOW, d), F32)])
    return pl.pallas_call(
        functools.partial(_moe_kernel, cap=cap, ne=ne, nt=nt),
        out_shape=jax.ShapeDtypeStruct((b, l, d), F32),
        grid_spec=grid_spec,
        compiler_params=_params("parallel", "arbitrary"),
        name="moe",
    )(cnt, hx, slot_row, aff_row, slot_t, wg, wu, wd, x1, mods, g_final.reshape(1, d), expand)


def kernel(x, c, ctx, c_ctx, w_ada, b_ada, g_norm1, w_in, b_gates, w_conv, w_fourier, w_mlstm,
           w_out, g_norm2, w_router, w_gate_e, w_up_e, w_down_e, g_final):
    assert w_ada.shape[0] == 1, "single-layer stack"
    b, l, d = x.shape
    lc = ctx.shape[1]
    d_f = d // 2
    d_m = d // 2
    dh = d_m // N_H
    ne = w_router.shape[2]
    cap = CAP_FACTOR * l // ne
    off_qk = d_f
    off_v = off_qk + 2 * d_m
    off_o = off_v + d_m
    off_g = off_o + d_m
    off_br = off_g + N_GATE_KINDS * N_H

    n_c = b + 1
    n_pad = -n_c % 8
    c_all = jnp.concatenate([c, c_ctx[None, :], jnp.zeros((n_pad, d), F32)], axis=0)
    mods = _ada(c_all, w_ada[0], b_ada[0]).reshape(n_c + n_pad, N_MOD, d)

    w = w_in[0]
    wb = w.astype(BF16)
    w_f, w_qk, w_v, w_o, w_br = (wb[:, 0:off_qk], wb[:, off_qk:off_v], wb[:, off_v:off_o],
                                 wb[:, off_o:off_g], wb[:, off_br:])
    w_gt = wb[:, off_g:off_br].T
    w9 = w_conv[0].reshape(9, 2 * d_m)
    qk_scale = jnp.concatenate([jnp.ones((1, d_m), F32), jnp.full((1, d_m), dh ** -0.5, F32)], axis=1)

    pc_qk, pc_v, gc_t = _inproj(ctx, mods, lambda bi, i: (b, 0, 0), g_norm1[0],
                                [w_qk, w_v], w_gt, tm=lc)
    q_c = _conv(pc_qk, w9, qk_scale, 1, 0, d_m, False)
    kt_c = _conv(pc_qk, w9, qk_scale, 1, d_m, 2 * d_m, True)
    rr_c = _gateprep(gc_t, b_gates[0], min(SCAN_CHUNK, lc))
    c0 = jnp.zeros((b, 2, N_H, dh, 2 * dh), F32)
    m0 = jnp.zeros((b, 2, N_H, 1, LANES), F32)
    c_seed, m_seed = _scan(q_c, kt_c, pc_v, None, rr_c, c0, m0, want_h=False, want_state=True)

    p_f, p_qk, p_v, p_o, p_br, g_t = _inproj(x, mods, lambda bi, i: (bi, 0, 0), g_norm1[0],
                                             [w_f, w_qk, w_v, w_o, w_br], w_gt, tm=512)
    q_x = _conv(p_qk, w9, qk_scale, l // GRID_W, 0, d_m, False)
    kt_x = _conv(p_qk, w9, qk_scale, l // GRID_W, d_m, 2 * d_m, True)
    rr_x = _gateprep(g_t, b_gates[0], SCAN_CHUNK)
    (hm,) = _scan(q_x, kt_x, p_v, p_o, rr_x, c_seed, m_seed, want_h=True, want_state=False)
    fu = _fourier(p_f)
    x1, hx2, aff_t = _merge(fu, hm, p_br, x, mods, w_fourier[0].astype(BF16),
                            w_mlstm[0].astype(BF16), w_out[0].astype(BF16), g_norm2[0], w_router[0],
                            tm=512)

    slot_t, before_t = _topk(aff_t.reshape(b * ne, l), cap)
    return _moe(hx2, slot_t.reshape(b, ne, l), before_t.reshape(b, ne, l), aff_t,
                w_gate_e[0].astype(BF16), w_up_e[0].astype(BF16), w_down_e[0].astype(BF16),
                x1, mods, g_final, cap)
```

```python
import functools
import math

import jax
import jax.numpy as jnp
from jax import lax
from jax.experimental import pallas as pl
from jax.experimental.pallas import tpu as pltpu

F32 = jnp.float32
BF16 = jnp.bfloat16

GRID_W = 64
N_FG = 4
N_H = 4
N_GATE_KINDS = 4
N_EXPERTS = 16
CAP_FACTOR = 2
EPS = 1e-6
LOG2E = math.log2(math.e)
N_MOD = 6

LANES = 128
V7X_VMEM_BYTES = 64 * 1024 * 1024
VMEM_LIMIT = V7X_VMEM_BYTES - 8 * 1024 * 1024

SCAN_CHUNK = 256
N_SCAN_ROWS = 8
N_TIE_EXTRACT = 3
MERGE_SUBTILE = 256
MOE_GROUP = 2
MOE_TILE = 256
MOE_WINDOW = 64
MOE_GATHER_WINDOW = 128
ROW_ALIGN = 8
SLOT_ALIGN = 16


def _params(*sem):
    return pltpu.CompilerParams(dimension_semantics=sem, vmem_limit_bytes=VMEM_LIMIT)


def _silu(v):
    return v * jax.nn.sigmoid(v)


def _rms_mod(v, g, shift, scale):
    ms = jnp.mean(v * v, axis=-1, keepdims=True)
    return (v * lax.rsqrt(ms + EPS) * g) * (1.0 + scale) + shift


def _ada_kernel(c_ref, w_ref, b_ref, o_ref):
    s = _silu(c_ref[...])
    o_ref[...] = jnp.dot(s, w_ref[...], preferred_element_type=F32,
                         precision=lax.Precision.HIGHEST) + b_ref[...]


def _ada(c_all, w_ada, b_ada):
    n, d = c_all.shape
    cols = w_ada.shape[1]
    tn = 512
    return pl.pallas_call(
        _ada_kernel,
        out_shape=jax.ShapeDtypeStruct((n, cols), F32),
        grid=(cols // tn,),
        in_specs=[pl.BlockSpec((n, d), lambda j: (0, 0)),
                  pl.BlockSpec((d, tn), lambda j: (0, j)),
                  pl.BlockSpec((1, tn), lambda j: (0, j))],
        out_specs=pl.BlockSpec((n, tn), lambda j: (0, j)),
        compiler_params=_params("parallel"),
        name="ada",
    )(c_all, w_ada, b_ada.reshape(1, cols))


def _inproj_kernel(x_ref, mod_ref, g_ref, *refs, n_seg):
    w_refs = refs[:n_seg]
    wgt_ref = refs[n_seg]
    o_refs = refs[n_seg + 1:2 * n_seg + 1]
    gt_ref = refs[2 * n_seg + 1]
    h = _rms_mod(x_ref[0], g_ref[...], mod_ref[0, 0:1, :], mod_ref[0, 1:2, :])
    hb = h.astype(BF16)
    for w_ref, o_ref in zip(w_refs, o_refs):
        o_ref[0] = jnp.dot(hb, w_ref[...], preferred_element_type=F32).astype(o_ref.dtype)
    gt_ref[0] = lax.dot_general(wgt_ref[...], hb, (((1,), (1,)), ((), ())),
                                preferred_element_type=F32)


def _inproj(x, mods, mod_row, g, w_segs, w_gt, tm):
    b, l, d = x.shape
    tm = min(tm, l)
    n_seg = len(w_segs)
    ng = w_gt.shape[0]
    const2 = lambda bi, i: (0, 0)
    in_specs = [pl.BlockSpec((1, tm, d), lambda bi, i: (bi, i, 0)),
                pl.BlockSpec((1, N_MOD, d), mod_row),
                pl.BlockSpec((1, d), const2)]
    in_specs += [pl.BlockSpec(w.shape, const2, pipeline_mode=pl.Buffered(1)) for w in w_segs]
    in_specs += [pl.BlockSpec(w_gt.shape, const2)]
    out_shape = [jax.ShapeDtypeStruct((b, l, w.shape[1]), BF16) for w in w_segs]
    out_shape += [jax.ShapeDtypeStruct((b, ng, l), F32)]
    out_specs = [pl.BlockSpec((1, tm, w.shape[1]), lambda bi, i: (bi, i, 0)) for w in w_segs]
    out_specs += [pl.BlockSpec((1, ng, tm), lambda bi, i: (bi, 0, i))]
    return pl.pallas_call(
        functools.partial(_inproj_kernel, n_seg=n_seg),
        out_shape=out_shape,
        grid=(b, l // tm),
        in_specs=in_specs,
        out_specs=out_specs,
        compiler_params=_params("parallel", "parallel"),
        name="inproj",
    )(x, mods, g.reshape(1, d), *w_segs, w_gt)


def _conv_kernel(p_ref, w_ref, s_ref, o_ref, *, rows, width, transpose):
    v = p_ref[0].astype(F32)
    l = v.shape[0]
    pos = lax.broadcasted_iota(jnp.int32, v.shape, 0)
    col = pos % width
    vl = jnp.where(col == 0, 0.0, pltpu.roll(v, 1, 0))
    vr = jnp.where(col == width - 1, 0.0, pltpu.roll(v, l - 1, 0))
    w = w_ref[...]

    def tap(dr):
        return (vl * w[3 * dr:3 * dr + 1, :] + v * w[3 * dr + 1:3 * dr + 2, :]
                + vr * w[3 * dr + 2:3 * dr + 3, :])

    acc = tap(1)
    if rows > 1:
        acc = acc + jnp.where(pos < width, 0.0, pltpu.roll(tap(0), width, 0))
        acc = acc + jnp.where(pos >= l - width, 0.0, pltpu.roll(tap(2), l - width, 0))
    y = _silu(acc) * s_ref[...]
    o_ref[0] = (y.T if transpose else y).astype(o_ref.dtype)


def _conv(p, w9, scale, rows, col_lo, col_hi, transpose):
    b, l, _ = p.shape
    ct = 256
    j0 = col_lo // ct
    n = col_hi - col_lo
    if transpose:
        out_shape = jax.ShapeDtypeStruct((b, n, l), BF16)
        out_spec = pl.BlockSpec((1, ct, l), lambda bi, j: (bi, j, 0))
    else:
        out_shape = jax.ShapeDtypeStruct((b, l, n), BF16)
        out_spec = pl.BlockSpec((1, l, ct), lambda bi, j: (bi, 0, j))
    return pl.pallas_call(
        functools.partial(_conv_kernel, rows=rows, width=l // rows, transpose=transpose),
        out_shape=out_shape,
        grid=(b, n // ct),
        in_specs=[pl.BlockSpec((1, l, ct), lambda bi, j: (bi, 0, j + j0)),
                  pl.BlockSpec((9, ct), lambda bi, j: (0, j + j0)),
                  pl.BlockSpec((1, ct), lambda bi, j: (0, j + j0))],
        out_specs=out_spec,
        compiler_params=_params("parallel", "parallel"),
        name="conv_kt" if transpose else "conv_q",
    )(p, w9, scale)


def _log_sigmoid(v):
    return jnp.minimum(v, 0.0) - jnp.log1p(jnp.exp(-jnp.abs(v)))


def _chunk_scan(v, op, ident, chunk, reverse):
    l = v.shape[1]
    pos = lax.broadcasted_iota(jnp.int32, v.shape, 1) % chunk
    s = 1
    while s < chunk:
        if reverse:
            shifted = jnp.where(pos < chunk - s, pltpu.roll(v, l - s, 1), ident)
        else:
            shifted = jnp.where(pos >= s, pltpu.roll(v, s, 1), ident)
        v = op(v, shifted)
        s *= 2
    return v


def _gateprep_kernel(g_ref, b_ref, o_ref, *, chunk):
    g = g_ref[0] + b_ref[...]
    i_f = g[0:N_H] * LOG2E
    f_f = _log_sigmoid(g[N_H:2 * N_H]) * LOG2E
    i_b = g[2 * N_H:3 * N_H] * LOG2E
    f_b = _log_sigmoid(g[3 * N_H:4 * N_H]) * LOG2E
    neg = -jnp.inf
    cum_f = _chunk_scan(f_f, jnp.add, 0.0, chunk, False)
    br_f = i_f - cum_f
    pm_f = _chunk_scan(br_f, jnp.maximum, neg, chunk, False)
    cum_b = _chunk_scan(f_b, jnp.add, 0.0, chunk, True)
    br_b = i_b - cum_b
    pm_b = _chunk_scan(br_b, jnp.maximum, neg, chunk, True)
    zero = jnp.zeros_like(cum_f[0:1])
    for h in range(N_H):
        o_ref[0, h] = jnp.concatenate(
            [a[h:h + 1] for a in (cum_f, br_f, pm_f, cum_b, br_b, pm_b)] + [zero, zero], axis=0)


def _gateprep(g_t, b_gates, chunk):
    b, ng, l = g_t.shape
    return pl.pallas_call(
        functools.partial(_gateprep_kernel, chunk=chunk),
        out_shape=jax.ShapeDtypeStruct((b, N_H, N_SCAN_ROWS, l), F32),
        grid=(b,),
        in_specs=[pl.BlockSpec((1, ng, l), lambda bi: (bi, 0, 0)),
                  pl.BlockSpec((ng, 1), lambda bi: (0, 0))],
        out_specs=pl.BlockSpec((1, N_H, N_SCAN_ROWS, l), lambda bi: (bi, 0, 0, 0)),
        compiler_params=_params("parallel"),
        name="gateprep",
    )(g_t, b_gates.reshape(ng, 1))


def _scan_kernel(*refs, chunk, want_h, want_state):
    q_ref, kt_ref, v_ref = refs[:3]
    pos = 3
    if want_h:
        og_ref = refs[pos]
        pos += 1
    rr_ref, c0_ref, m0_ref = refs[pos:pos + 3]
    pos += 3
    if want_h:
        h_ref = refs[pos]
        pos += 1
    if want_state:
        cs_ref, ms_ref = refs[pos:pos + 2]
        pos += 2
    vext_scr = refs[pos]
    if want_h:
        hf_scr, hb_scr = refs[pos + 1:pos + 3]

    l, dh = q_ref.shape[1], q_ref.shape[2]
    nc = l // chunk
    rep = chunk // dh
    vext_scr[:, 0:dh] = v_ref[0]
    vext_scr[:, dh:2 * dh] = jnp.ones((l, dh), BF16)

    jj = lax.broadcasted_iota(jnp.int32, (chunk, chunk), 0)
    ss = lax.broadcasted_iota(jnp.int32, (chunk, chunk), 1)
    causal = (jj >= ss, jj <= ss)

    cols = rr_ref[0, 0].T
    cst = [c0_ref[0, d, 0] for d in range(2)]
    mst = [m0_ref[0, d, 0][:, 0:1] for d in range(2)]

    for i in range(nc):
        for d in range(2):
            c = i if d == 0 else nc - 1 - i
            lo = c * chunk
            end = chunk - 1 if d == 0 else 0
            r0 = 3 * d
            qc = q_ref[0, lo:lo + chunk, :]
            ktc = kt_ref[0, :, lo:lo + chunk]
            vc = vext_scr[lo:lo + chunk, :]
            cum_c = cols[lo:lo + chunk, r0:r0 + 1]
            pm_c = cols[lo:lo + chunk, r0 + 2:r0 + 3]
            br_r = rr_ref[0, 0, r0 + 1:r0 + 2, lo:lo + chunk]
            tot = rr_ref[0, 0, r0:r0 + 1, lo + end:lo + end + 1]
            pm_end = rr_ref[0, 0, r0 + 2:r0 + 3, lo + end:lo + end + 1]
            m0 = mst[d]
            mm = jnp.broadcast_to(jnp.maximum(m0, pm_c), (chunk, dh))
            d_in = jnp.where(causal[d], jnp.exp2(br_r - jnp.concatenate([mm] * rep, axis=1)), 0.0)
            d_st = jnp.exp2(m0 - mm)
            s = jnp.dot(qc, ktc, preferred_element_type=F32)
            p = (s * d_in).astype(BF16)
            qd = (qc.astype(F32) * d_st).astype(BF16)
            both = (jnp.dot(p, vc, preferred_element_type=F32)
                    + jnp.dot(qd, cst[d].astype(BF16), preferred_element_type=F32))
            if want_h:
                m_j = jnp.broadcast_to(cum_c, (chunk, dh)) + mm
                hc = both[:, 0:dh] / jnp.maximum(jnp.abs(both[:, dh:2 * dh]), jnp.exp2(-m_j))
                (hf_scr if d == 0 else hb_scr)[lo:lo + chunk, :] = hc
            kw = (ktc.astype(F32) * d_in[end:end + 1, :]).astype(BF16)
            upd = jnp.dot(kw, vc, preferred_element_type=F32)
            dn = d_st[end:end + 1, :]
            cst[d] = jnp.concatenate([dn, dn], axis=1) * cst[d] + upd
            mst[d] = tot + jnp.maximum(m0, pm_end)

    if want_h:
        gate = jax.nn.sigmoid(og_ref[0].astype(F32))
        h_ref[0] = (gate * (hf_scr[...] + hb_scr[...])).astype(h_ref.dtype)
    if want_state:
        for d in range(2):
            cs_ref[0, d, 0] = cst[d]
            ms_ref[0, d, 0] = jnp.broadcast_to(mst[d], (1, LANES))


def _scan(q, kt, v, og, rr, c0, m0, *, want_h, want_state):
    b, l, dm = v.shape
    dh = dm // N_H
    chunk = min(SCAN_CHUNK, l)
    assert chunk % dh == 0 and l % chunk == 0
    tok = pl.BlockSpec((1, l, dh), lambda bi, h: (bi, 0, h))
    in_specs = [tok, pl.BlockSpec((1, dh, l), lambda bi, h: (bi, h, 0)), tok]
    args = [q, kt, v]
    if want_h:
        in_specs.append(tok)
        args.append(og)
    st_c = pl.BlockSpec((1, 2, 1, dh, 2 * dh), lambda bi, h: (bi, 0, h, 0, 0))
    st_m = pl.BlockSpec((1, 2, 1, 1, LANES), lambda bi, h: (bi, 0, h, 0, 0))
    in_specs += [pl.BlockSpec((1, 1, N_SCAN_ROWS, l), lambda bi, h: (bi, h, 0, 0)), st_c, st_m]
    args += [rr, c0, m0]
    out_shape, out_specs = [], []
    scratch = [pltpu.VMEM((l, 2 * dh), BF16)]
    if want_h:
        out_shape.append(jax.ShapeDtypeStruct((b, l, dm), BF16))
        out_specs.append(tok)
        scratch += [pltpu.VMEM((l, dh), F32), pltpu.VMEM((l, dh), F32)]
    if want_state:
        out_shape += [jax.ShapeDtypeStruct(c0.shape, F32), jax.ShapeDtypeStruct(m0.shape, F32)]
        out_specs += [st_c, st_m]
    return pl.pallas_call(
        functools.partial(_scan_kernel, chunk=chunk, want_h=want_h, want_state=want_state),
        out_shape=out_shape,
        grid=(b, N_H),
        in_specs=in_specs,
        out_specs=out_specs,
        scratch_shapes=scratch,
        compiler_params=_params("parallel", "parallel"),
        name="scan_h" if want_h else "scan_state",
    )(*args)


def _twiddle(z, k, n):
    zr, zi = z
    k %= n
    if k == 0:
        return z
    if 4 * k == n:
        return zi, -zr
    if 2 * k == n:
        return -zr, -zi
    if 4 * k == 3 * n:
        return -zi, zr
    wr, wi = math.cos(2.0 * math.pi * k / n), -math.sin(2.0 * math.pi * k / n)
    return zr * wr - zi * wi, zr * wi + zi * wr


def _fft_list(xs):
    n = len(xs)
    if n == 1:
        return xs
    ev, od = _fft_list(xs[0::2]), _fft_list(xs[1::2])
    out = [None] * n
    for k in range(n // 2):
        tr, ti = _twiddle(od[k], k, n)
        out[k] = (ev[k][0] + tr, ev[k][1] + ti)
        out[k + n // 2] = (ev[k][0] - tr, ev[k][1] - ti)
    return out


def _fourier_kernel(u_ref, wc_ref, tw_ref, cs_ref, o_ref, zz_scr, r_scr, *, fg, n1):
    l, df = u_ref.shape[1], u_ref.shape[2]
    n2 = l // n1
    for g in range(df // fg):
        r = jnp.dot(u_ref[0, :, g * fg:(g + 1) * fg], wc_ref[...], preferred_element_type=F32)
        xs = [(r[j * n2:(j + 1) * n2, 0:fg], r[j * n2:(j + 1) * n2, fg:2 * fg]) for j in range(n1)]
        zs = _fft_list(xs)
        for k1 in range(n1):
            zr, zi = zs[k1]
            if k1 > 0:
                tc, ts = tw_ref[0, k1], tw_ref[1, k1]
                zr, zi = zr * tc + zi * ts, zi * tc - zr * ts
            zz_scr[k1, 0:n2, g * fg:(g + 1) * fg] = zr.astype(BF16)
            zz_scr[k1, n2:2 * n2, g * fg:(g + 1) * fg] = zi.astype(BF16)
    pitch = r_scr.shape[1] // n1
    for k1 in range(n1):
        res = jnp.dot(cs_ref[...], zz_scr[k1], preferred_element_type=F32)
        for c in range(df // LANES):
            r_scr[c, k1 * pitch:k1 * pitch + n2, :] = res[:, c * LANES:(c + 1) * LANES]

    def emit(j, carry):
        for c in range(df // LANES):
            a = r_scr[c, pl.ds(2 * j, n1, stride=pitch), :]
            b = r_scr[c, pl.ds(2 * j + 1, n1, stride=pitch), :]
            o_ref[0, pl.ds(pl.multiple_of(2 * n1 * j, 2 * n1), 2 * n1), c * LANES:(c + 1) * LANES] = (
                jnp.concatenate([a, b], axis=0).astype(o_ref.dtype))
        return carry

    lax.fori_loop(0, n2 // 2, emit, 0, unroll=8)


def _dft_angles(rows, cols, n):
    ang = ((rows[:, None] * cols[None, :]) % n).astype(F32) * (2.0 * math.pi / n)
    return jnp.cos(ang), jnp.sin(ang)


def _fourier(u):
    b, l, df = u.shape
    fg = df // N_FG
    n1 = 8
    n2 = l // n1
    ar = lambda n: jnp.arange(n, dtype=jnp.int32)
    cc, sc = _dft_angles(ar(fg), ar(fg), fg)
    wc = (jnp.concatenate([cc, -sc], axis=1) * fg ** -0.5).astype(BF16)
    tc, ts = _dft_angles(ar(n1), ar(n2), l)
    tw = jnp.broadcast_to(jnp.stack([tc, ts])[..., None], (2, n1, n2, fg))
    c2, s2 = _dft_angles(ar(n2), ar(n2), n2)
    cs = (jnp.concatenate([c2, s2], axis=1) * l ** -0.5).astype(BF16)
    const = lambda a: pl.BlockSpec(a.shape, lambda bi: (0,) * a.ndim, pipeline_mode=pl.Buffered(1))
    pitch = n2 + 4
    return pl.pallas_call(
        functools.partial(_fourier_kernel, fg=fg, n1=n1),
        out_shape=jax.ShapeDtypeStruct((b, l, df), BF16),
        grid=(b,),
        in_specs=[pl.BlockSpec((1, l, df), lambda bi: (bi, 0, 0)), const(wc), const(tw), const(cs)],
        out_specs=pl.BlockSpec((1, l, df), lambda bi: (bi, 0, 0)),
        scratch_shapes=[pltpu.VMEM((n1, 2 * n2, df), BF16),
                        pltpu.VMEM((df // LANES, n1 * pitch, LANES), F32)],
        compiler_params=_params("parallel"),
        name="fourier",
    )(u, wc, tw, cs)


def _merge_kernel(fu_ref, hm_ref, br_ref, x_ref, mod_ref, wf_ref, wm_ref, wo_ref, g2_ref,
                  wr1_ref, wr2_ref, x1_ref, hx_ref, aff_ref):
    tm, d = x_ref.shape[1], x_ref.shape[2]
    ne = aff_ref.shape[1]
    for r0 in range(0, tm, MERGE_SUBTILE):
        rows = slice(r0, r0 + MERGE_SUBTILE)
        yf = jnp.dot(fu_ref[0, rows, :], wf_ref[...], preferred_element_type=F32)
        ym = jnp.dot(hm_ref[0, rows, :], wm_ref[...], preferred_element_type=F32)
        gf = jax.nn.sigmoid(br_ref[0, rows, 0:d].astype(F32))
        gm = jax.nn.sigmoid(br_ref[0, rows, d:2 * d].astype(F32))
        merged = (gf * yf + gm * ym).astype(BF16)
        o = jnp.dot(merged, wo_ref[...], preferred_element_type=F32)
        x1 = x_ref[0, rows, :] + mod_ref[0, 2:3, :] * o
        x1_ref[0, rows, :] = x1
        h2 = _rms_mod(x1, g2_ref[...], mod_ref[0, 3:4, :], mod_ref[0, 4:5, :])
        hi = h2.astype(BF16)
        hx_ref[0, rows, :] = hi
        lo = (h2 - hi.astype(F32)).astype(BF16)
        r = (jnp.dot(hi, wr1_ref[...], preferred_element_type=F32)
             + jnp.dot(lo, wr2_ref[...], preferred_element_type=F32))
        rt = r.T
        logits = rt[0:ne, :] + rt[ne:2 * ne, :]
        e = jnp.exp(logits - jnp.max(logits, axis=0, keepdims=True))
        aff_ref[0, :, rows] = e / jnp.sum(e, axis=0, keepdims=True)


def _merge(fu, hm, pbr, x, mods, wf, wm, wo, g2, w_router, tm):
    b, l, d = x.shape
    tm = min(tm, l)
    ne = w_router.shape[1]
    whi = w_router.astype(BF16)
    wlo = (w_router - whi.astype(F32)).astype(BF16)
    pad = jnp.zeros((d, LANES - 2 * ne), BF16)
    wr1 = jnp.concatenate([whi, wlo, pad], axis=1)
    wr2 = jnp.concatenate([whi, jnp.zeros((d, ne), BF16), pad], axis=1)
    tok = lambda n: pl.BlockSpec((1, tm, n), lambda bi, i: (bi, i, 0))
    const = lambda a: pl.BlockSpec(a.shape, lambda bi, i: (0, 0), pipeline_mode=pl.Buffered(1))
    return pl.pallas_call(
        _merge_kernel,
        out_shape=[jax.ShapeDtypeStruct((b, l, d), F32),
                   jax.ShapeDtypeStruct((b, l, d), BF16),
                   jax.ShapeDtypeStruct((b, ne, l), F32)],
        grid=(b, l // tm),
        in_specs=[tok(fu.shape[2]), tok(hm.shape[2]), tok(pbr.shape[2]), tok(d),
                  pl.BlockSpec((1, N_MOD, d), lambda bi, i: (bi, 0, 0)),
                  const(wf), const(wm), const(wo),
                  pl.BlockSpec((1, d), lambda bi, i: (0, 0)),
                  const(wr1), const(wr2)],
        out_specs=[tok(d), tok(d), pl.BlockSpec((1, ne, tm), lambda bi, i: (bi, 0, i))],
        compiler_params=_params("parallel", "parallel"),
        name="merge",
    )(fu, hm, pbr, x, mods, wf, wm, wo, g2.reshape(1, d), wr1, wr2)


def _topk_kernel(a_ref, u_ref, slot_ref, before_ref, *, k):
    a = a_ref[...]
    l = a.shape[1]
    total = lambda m: jnp.sum(m, axis=1, keepdims=True)
    ge = lambda bits: jnp.where(a >= pltpu.bitcast(bits, F32), 1.0, 0.0)

    def body(i, t):
        cand = t | lax.shift_left(jnp.int32(1), 30 - i)
        return jnp.where(total(ge(cand)) >= k, cand, t)

    t = lax.fori_loop(0, 31, body, jnp.zeros((a.shape[0], 1), jnp.int32))
    chosen = ge(t + 1)
    rem = ge(t) - chosen
    need = k - total(chosen)
    idx = lax.broadcasted_iota(jnp.int32, a.shape, 1).astype(F32)
    for _ in range(N_TIE_EXTRACT):
        cur = jnp.where(rem > 0.0, a, -1.0)
        is_mx = jnp.where(cur == jnp.max(cur, axis=1, keepdims=True), rem, 0.0)
        first = jnp.min(jnp.where(is_mx > 0.0, idx, float(l)), axis=1, keepdims=True)
        pick = jnp.where(idx == first, jnp.where(need > 0.0, 1.0, 0.0), 0.0)
        chosen = chosen + pick
        rem = rem - pick
        need = need - total(pick)
    before = jnp.dot(rem.astype(BF16), u_ref[...], preferred_element_type=F32)
    sel = chosen + jnp.where(before < need, rem, 0.0)
    rank = jnp.dot(sel.astype(BF16), u_ref[...], preferred_element_type=F32)
    slot_ref[...] = jnp.where(sel > 0.0, rank, -1.0)
    before_ref[...] = rank


def _topk(aff_t, k):
    r, l = aff_t.shape
    tr = min(128, r)
    idx = jnp.arange(l, dtype=jnp.int32)
    upper = (idx[:, None] < idx[None, :]).astype(BF16)
    return pl.pallas_call(
        functools.partial(_topk_kernel, k=k),
        out_shape=[jax.ShapeDtypeStruct((r, l), F32)] * 2,
        grid=(r // tr,),
        in_specs=[pl.BlockSpec((tr, l), lambda i: (i, 0)),
                  pl.BlockSpec((l, l), lambda i: (0, 0), pipeline_mode=pl.Buffered(1))],
        out_specs=[pl.BlockSpec((tr, l), lambda i: (i, 0))] * 2,
        compiler_params=_params("parallel"),
        name="topk",
    )(aff_t, upper)


def _moe_kernel(cnt_ref, hx_ref, srow_ref, arow_ref, slot_ref, wg_ref, wu_ref, wd_ref, x1_ref,
                mod_ref, g_ref, xp_ref, o_ref, y_scr, xs_scr, *, cap, ne, nt):
    bi = pl.program_id(0)
    s = pl.program_id(1)
    ng = hx_ref.shape[0]
    win = MOE_WINDOW
    per_sample = ne * (nt + 1)

    @pl.when(s == 0)
    def _():
        for gi in range(ng):
            y_scr[gi, ne * cap:ne * cap + win, :] = jnp.zeros((win, y_scr.shape[2]), BF16)

    @pl.when(s < ne)
    def _():
        l = hx_ref.shape[1]
        tb, gw = MOE_TILE, MOE_GATHER_WINDOW
        xs_scr[...] = jnp.zeros(xs_scr.shape, F32)
        row_i = lax.broadcasted_iota(jnp.int32, (gw, tb), 0)
        lows, n_pass = [], 0
        for gi in range(ng):
            base = (bi * ng + gi) * per_sample + s * (nt + 1)
            for j in range(nt):
                low = (cnt_ref[base + j] // ROW_ALIGN) * ROW_ALIGN
                lows.append(low)
                n_pass = jnp.maximum(n_pass, (cnt_ref[base + j + 1] - low + gw - 1) // gw)

        def gather_pass(w, carry):
            for gi in range(ng):
                for j in range(nt):
                    lo = pl.multiple_of(jnp.minimum(lows[gi * nt + j] + w * gw, cap), ROW_ALIGN)
                    srow_blk = srow_ref[gi, 0, :, j * tb:(j + 1) * tb]
                    p = jnp.where(srow_blk == (row_i + lo).astype(F32), 1.0, 0.0).astype(BF16)
                    xs_scr[gi, pl.ds(lo, gw), :] += jnp.dot(
                        p, hx_ref[gi, j * tb:(j + 1) * tb, :], preferred_element_type=F32)
            return carry

        gather_pass(0, 0)
        lax.fori_loop(1, n_pass, gather_pass, 0)
        xs = jnp.concatenate([xs_scr[gi, 0:cap, :] for gi in range(ng)], axis=0).astype(BF16)
        gate = jnp.dot(xs, wg_ref[0], preferred_element_type=F32)
        up = jnp.dot(xs, wu_ref[0], preferred_element_type=F32)
        hid = (_silu(gate) * up).astype(BF16)
        y = jnp.dot(hid, wd_ref[0], preferred_element_type=F32)
        slot_i = lax.broadcasted_iota(jnp.int32, (cap, l), 0).astype(F32)
        for gi in range(ng):
            hit = srow_ref[gi, 0] == slot_i
            vals = jnp.sum(jnp.where(hit, arow_ref[gi, 0], 0.0), axis=1, keepdims=True)
            y_scr[gi, pl.ds(pl.multiple_of(s * cap, cap), cap), :] = (
                y[gi * cap:(gi + 1) * cap, :] * vals).astype(BF16)

    @pl.when(s >= ne)
    def _():
        tt, d = x1_ref.shape[1], x1_ref.shape[2]
        gi = (s - ne) // nt
        base = (bi * ng + gi) * per_sample + (s - ne) % nt
        lows, n_pass = [], 0
        for e in range(ne):
            start = cnt_ref[base + e * (nt + 1)]
            end = cnt_ref[base + e * (nt + 1) + 1]
            low = (start // SLOT_ALIGN) * SLOT_ALIGN
            lows.append(low)
            n_pass = jnp.maximum(n_pass, (end - low + win - 1) // win)
        sexp = jnp.dot(slot_ref[0].T.astype(BF16), xp_ref[...], preferred_element_type=F32)
        lane = lax.broadcasted_iota(jnp.int32, (1, ne * win), 1)
        grp, off = lane // win, lane % win

        def one_pass(w, acc):
            tgt, rows = off, []
            for e in range(ne):
                lo = jnp.minimum(lows[e] + w * win, cap)
                tgt = jnp.where(grp == e, off + lo, tgt)
                rows.append(y_scr[gi, pl.ds(pl.multiple_of(e * cap + lo, SLOT_ALIGN), win), :])
            pt = jnp.where(sexp == tgt.astype(F32), 1.0, 0.0).astype(BF16)
            return acc + jnp.dot(pt, jnp.concatenate(rows, axis=0), preferred_element_type=F32)

        moe = lax.fori_loop(1, n_pass, one_pass, one_pass(0, jnp.zeros((tt, d), F32)))
        v = x1_ref[0] + mod_ref[gi, 5:6, :] * moe
        ms = jnp.mean(v * v, axis=-1, keepdims=True)
        o_ref[0] = v * lax.rsqrt(ms + EPS) * g_ref[...]


def _moe(hx, slot_t, before_t, aff_t, wg, wu, wd, x1, mods, g_final, cap):
    b, l, d = hx.shape
    ne, _, f = wg.shape
    tt = MOE_TILE
    nt = l // tt
    win = MOE_WINDOW
    ng = MOE_GROUP if b % MOE_GROUP == 0 else 1
    assert cap % SLOT_ALIGN == 0 and win % SLOT_ALIGN == 0 and (win & (win - 1)) == 0
    slot_row = slot_t.reshape(b, ne, 1, l)
    aff_row = aff_t.reshape(b, ne, 1, l)
    cnt = jnp.concatenate([before_t[:, :, ::tt], jnp.full((b, ne, 1), cap, F32)], axis=2)
    cnt = cnt.astype(jnp.int32).reshape(b * ne * (nt + 1))
    lane = jnp.arange(ne * win, dtype=jnp.int32)
    expand = (lane[None, :] // win == jnp.arange(ne, dtype=jnp.int32)[:, None]).astype(BF16)
    ex = lambda s: jnp.minimum(s, ne - 1)
    sample = lambda bi, s: bi * ng + jnp.maximum(s - ne, 0) // nt
    tile = lambda s: jnp.maximum(s - ne, 0) % nt
    row = pl.BlockSpec((ng, 1, 1, l), lambda bi, s, c: (bi, ex(s), 0, 0))
    wspec = lambda k, n: pl.BlockSpec((1, k, n), lambda bi, s, c: (ex(s), 0, 0))
    grid_spec = pltpu.PrefetchScalarGridSpec(
        num_scalar_prefetch=1,
        grid=(b // ng, ne + ng * nt),
        in_specs=[pl.BlockSpec((ng, l, d), lambda bi, s, c: (bi, 0, 0),
                               pipeline_mode=pl.Buffered(1)),
                  row, row,
                  pl.BlockSpec((1, ne, tt), lambda bi, s, c: (sample(bi, s), 0, tile(s))),
                  wspec(d, f), wspec(d, f), wspec(f, d),
                  pl.BlockSpec((1, tt, d), lambda bi, s, c: (sample(bi, s), tile(s), 0)),
                  pl.BlockSpec((ng, N_MOD, d), lambda bi, s, c: (bi, 0, 0)),
                  pl.BlockSpec((1, d), lambda bi, s, c: (0, 0)),
                  pl.BlockSpec((ne, ne * win), lambda bi, s, c: (0, 0))],
        out_specs=pl.BlockSpec((1, tt, d), lambda bi, s, c: (sample(bi, s), tile(s), 0)),
        scratch_shapes=[pltpu.VMEM((ng, ne * cap + win, d), BF16),
                        pltpu.VMEM((ng, cap + MOE_GATHER_WINDOW, d), F32)])
    return pl.pallas_call(
        functools.partial(_moe_kernel, cap=cap, ne=ne, nt=nt),
        out_shape=jax.ShapeDtypeStruct((b, l, d), F32),
        grid_spec=grid_spec,
        compiler_params=_params("parallel", "arbitrary"),
        name="moe",
    )(cnt, hx, slot_row, aff_row, slot_t, wg, wu, wd, x1, mods, g_final.reshape(1, d), expand)


def kernel(x, c, ctx, c_ctx, w_ada, b_ada, g_norm1, w_in, b_gates, w_conv, w_fourier, w_mlstm,
           w_out, g_norm2, w_router, w_gate_e, w_up_e, w_down_e, g_final):
    assert w_ada.shape[0] == 1, "single-layer stack"
    b, l, d = x.shape
    lc = ctx.shape[1]
    d_f = d // 2
    d_m = d // 2
    dh = d_m // N_H
    ne = w_router.shape[2]
    cap = CAP_FACTOR * l // ne
    off_qk = d_f
    off_v = off_qk + 2 * d_m
    off_o = off_v + d_m
    off_g = off_o + d_m
    off_br = off_g + N_GATE_KINDS * N_H

    n_c = b + 1
    n_pad = -n_c % 8
    c_all = jnp.concatenate([c, c_ctx[None, :], jnp.zeros((n_pad, d), F32)], axis=0)
    mods = _ada(c_all, w_ada[0], b_ada[0]).reshape(n_c + n_pad, N_MOD, d)

    w = w_in[0]
    wb = w.astype(BF16)
    w_f, w_qk, w_v, w_o, w_br = (wb[:, 0:off_qk], wb[:, off_qk:off_v], wb[:, off_v:off_o],
                                 wb[:, off_o:off_g], wb[:, off_br:])
    w_gt = wb[:, off_g:off_br].T
    w9 = w_conv[0].reshape(9, 2 * d_m)
    qk_scale = jnp.concatenate([jnp.ones((1, d_m), F32), jnp.full((1, d_m), dh ** -0.5, F32)], axis=1)

    pc_qk, pc_v, gc_t = _inproj(ctx, mods, lambda bi, i: (b, 0, 0), g_norm1[0],
                                [w_qk, w_v], w_gt, tm=lc)
    q_c = _conv(pc_qk, w9, qk_scale, 1, 0, d_m, False)
    kt_c = _conv(pc_qk, w9, qk_scale, 1, d_m, 2 * d_m, True)
    rr_c = _gateprep(gc_t, b_gates[0], min(SCAN_CHUNK, lc))
    c0 = jnp.zeros((b, 2, N_H, dh, 2 * dh), F32)
    m0 = jnp.zeros((b, 2, N_H, 1, LANES), F32)
    c_seed, m_seed = _scan(q_c, kt_c, pc_v, None, rr_c, c0, m0, want_h=False, want_state=True)

    p_f, p_qk, p_v, p_o, p_br, g_t = _inproj(x, mods, lambda bi, i: (bi, 0, 0), g_norm1[0],
                                             [w_f, w_qk, w_v, w_o, w_br], w_gt, tm=1024)
    q_x = _conv(p_qk, w9, qk_scale, l // GRID_W, 0, d_m, False)
    kt_x = _conv(p_qk, w9, qk_scale, l // GRID_W, d_m, 2 * d_m, True)
    rr_x = _gateprep(g_t, b_gates[0], SCAN_CHUNK)
    (hm,) = _scan(q_x, kt_x, p_v, p_o, rr_x, c_seed, m_seed, want_h=True, want_state=False)
    fu = _fourier(p_f)
    x1, hx2, aff_t = _merge(fu, hm, p_br, x, mods, w_fourier[0].astype(BF16),
                            w_mlstm[0].astype(BF16), w_out[0].astype(BF16), g_norm2[0], w_router[0],
                            tm=1024)

    slot_t, before_t = _topk(aff_t.reshape(b * ne, l), cap)
    return _moe(hx2, slot_t.reshape(b, ne, l), before_t.reshape(b, ne, l), aff_t,
                w_gate_e[0].astype(BF16), w_up_e[0].astype(BF16), w_down_e[0].astype(BF16),
                x1, mods, g_final, cap)
```

```python
import functools
import math

import jax
import jax.numpy as jnp
from jax import lax
from jax.experimental import pallas as pl
from jax.experimental.pallas import tpu as pltpu

F32 = jnp.float32
BF16 = jnp.bfloat16

GRID_W = 64
N_FG = 4
N_H = 4
N_GATE_KINDS = 4
N_EXPERTS = 16
CAP_FACTOR = 2
EPS = 1e-6
LOG2E = math.log2(math.e)
N_MOD = 6

LANES = 128
V7X_VMEM_BYTES = 64 * 1024 * 1024
VMEM_LIMIT = V7X_VMEM_BYTES - 8 * 1024 * 1024

SCAN_CHUNK = 256
SCAN_HEADS_PER_STEP = 1
N_SCAN_ROWS = 8
N_TIE_EXTRACT = 3
MERGE_SUBTILE = 512
MOE_GROUP = 2
MOE_TILE = 256
MOE_TILES_PER_STEP = 2
MOE_WINDOW = 64
MOE_GATHER_WINDOW = 128
ROW_ALIGN = 8
SLOT_ALIGN = 16


def _params(*sem):
    return pltpu.CompilerParams(dimension_semantics=sem, vmem_limit_bytes=VMEM_LIMIT)


def _silu(v):
    return v * jax.nn.sigmoid(v)


def _rms_mod(v, g, shift, scale):
    ms = jnp.mean(v * v, axis=-1, keepdims=True)
    return (v * lax.rsqrt(ms + EPS) * g) * (1.0 + scale) + shift


def _ada_kernel(c_ref, w_ref, b_ref, o_ref):
    s = _silu(c_ref[...])
    o_ref[...] = jnp.dot(s, w_ref[...], preferred_element_type=F32,
                         precision=lax.Precision.HIGHEST) + b_ref[...]


def _ada(c_all, w_ada, b_ada):
    n, d = c_all.shape
    cols = w_ada.shape[1]
    tn = 512
    return pl.pallas_call(
        _ada_kernel,
        out_shape=jax.ShapeDtypeStruct((n, cols), F32),
        grid=(cols // tn,),
        in_specs=[pl.BlockSpec((n, d), lambda j: (0, 0)),
                  pl.BlockSpec((d, tn), lambda j: (0, j)),
                  pl.BlockSpec((1, tn), lambda j: (0, j))],
        out_specs=pl.BlockSpec((n, tn), lambda j: (0, j)),
        compiler_params=_params("parallel"),
        name="ada",
    )(c_all, w_ada, b_ada.reshape(1, cols))


def _inproj_kernel(x_ref, mod_ref, g_ref, *refs, n_seg):
    w_refs = refs[:n_seg]
    wgt_ref = refs[n_seg]
    o_refs = refs[n_seg + 1:2 * n_seg + 1]
    gt_ref = refs[2 * n_seg + 1]
    h = _rms_mod(x_ref[0], g_ref[...], mod_ref[0, 0:1, :], mod_ref[0, 1:2, :])
    hb = h.astype(BF16)
    for w_ref, o_ref in zip(w_refs, o_refs):
        o_ref[0] = jnp.dot(hb, w_ref[...], preferred_element_type=F32).astype(o_ref.dtype)
    gt_ref[0] = lax.dot_general(wgt_ref[...], hb, (((1,), (1,)), ((), ())),
                                preferred_element_type=F32)


def _inproj(x, mods, mod_row, g, w_segs, w_gt, tm):
    b, l, d = x.shape
    tm = min(tm, l)
    n_seg = len(w_segs)
    ng = w_gt.shape[0]
    const2 = lambda bi, i: (0, 0)
    in_specs = [pl.BlockSpec((1, tm, d), lambda bi, i: (bi, i, 0)),
                pl.BlockSpec((1, N_MOD, d), mod_row),
                pl.BlockSpec((1, d), const2)]
    in_specs += [pl.BlockSpec(w.shape, const2, pipeline_mode=pl.Buffered(1)) for w in w_segs]
    in_specs += [pl.BlockSpec(w_gt.shape, const2)]
    out_shape = [jax.ShapeDtypeStruct((b, l, w.shape[1]), BF16) for w in w_segs]
    out_shape += [jax.ShapeDtypeStruct((b, ng, l), F32)]
    out_specs = [pl.BlockSpec((1, tm, w.shape[1]), lambda bi, i: (bi, i, 0)) for w in w_segs]
    out_specs += [pl.BlockSpec((1, ng, tm), lambda bi, i: (bi, 0, i))]
    return pl.pallas_call(
        functools.partial(_inproj_kernel, n_seg=n_seg),
        out_shape=out_shape,
        grid=(b, l // tm),
        in_specs=in_specs,
        out_specs=out_specs,
        compiler_params=_params("parallel", "parallel"),
        name="inproj",
    )(x, mods, g.reshape(1, d), *w_segs, w_gt)


def _conv_kernel(p_ref, w_ref, s_ref, o_ref, *, rows, width, transpose):
    v = p_ref[0].astype(F32)
    l, ct = v.shape
    grid = lambda a: a.reshape(rows, width, ct)
    v3, vl3, vr3 = grid(v), grid(pltpu.roll(v, 1, 0)), grid(pltpu.roll(v, l - 1, 0))
    col = lax.broadcasted_iota(jnp.int32, (1, width, ct), 1)
    w = w_ref[...]
    wl = [jnp.where(col == 0, 0.0, w[3 * dr:3 * dr + 1, :][None]) for dr in range(3)]
    wr = [jnp.where(col == width - 1, 0.0, w[3 * dr + 2:3 * dr + 3, :][None]) for dr in range(3)]

    def tap(dr):
        return vl3 * wl[dr] + v3 * w[3 * dr + 1:3 * dr + 2, :][None] + vr3 * wr[dr]

    acc = tap(1)
    if rows > 1:
        edge = jnp.zeros((1, width, ct), F32)
        acc = acc + jnp.concatenate([edge, tap(0)[:-1]], axis=0)
        acc = acc + jnp.concatenate([tap(2)[1:], edge], axis=0)
    acc = acc.reshape(l, ct)
    y = _silu(acc) * s_ref[...]
    o_ref[0] = (y.T if transpose else y).astype(o_ref.dtype)


def _conv(p, w9, scale, rows, col_lo, col_hi, transpose):
    b, l, _ = p.shape
    ct = 512
    j0 = col_lo // ct
    n = col_hi - col_lo
    if transpose:
        out_shape = jax.ShapeDtypeStruct((b, n, l), BF16)
        out_spec = pl.BlockSpec((1, ct, l), lambda bi, j: (bi, j, 0))
    else:
        out_shape = jax.ShapeDtypeStruct((b, l, n), BF16)
        out_spec = pl.BlockSpec((1, l, ct), lambda bi, j: (bi, 0, j))
    return pl.pallas_call(
        functools.partial(_conv_kernel, rows=rows, width=l // rows, transpose=transpose),
        out_shape=out_shape,
        grid=(b, n // ct),
        in_specs=[pl.BlockSpec((1, l, ct), lambda bi, j: (bi, 0, j + j0)),
                  pl.BlockSpec((9, ct), lambda bi, j: (0, j + j0)),
                  pl.BlockSpec((1, ct), lambda bi, j: (0, j + j0))],
        out_specs=out_spec,
        compiler_params=_params("parallel", "parallel"),
        name="conv_kt" if transpose else "conv_q",
    )(p, w9, scale)


def _log_sigmoid(v):
    return jnp.minimum(v, 0.0) - jnp.log1p(jnp.exp(-jnp.abs(v)))


def _chunk_scan(v, op, ident, chunk, reverse):
    l = v.shape[1]
    pos = lax.broadcasted_iota(jnp.int32, v.shape, 1) % chunk
    s = 1
    while s < chunk:
        if reverse:
            shifted = jnp.where(pos < chunk - s, pltpu.roll(v, l - s, 1), ident)
        else:
            shifted = jnp.where(pos >= s, pltpu.roll(v, s, 1), ident)
        v = op(v, shifted)
        s *= 2
    return v


def _gateprep_kernel(g_ref, b_ref, o_ref, *, chunk):
    g = g_ref[0] + b_ref[...]
    i_f = g[0:N_H] * LOG2E
    f_f = _log_sigmoid(g[N_H:2 * N_H]) * LOG2E
    i_b = g[2 * N_H:3 * N_H] * LOG2E
    f_b = _log_sigmoid(g[3 * N_H:4 * N_H]) * LOG2E
    neg = -jnp.inf
    cum_f = _chunk_scan(f_f, jnp.add, 0.0, chunk, False)
    br_f = i_f - cum_f
    pm_f = _chunk_scan(br_f, jnp.maximum, neg, chunk, False)
    cum_b = _chunk_scan(f_b, jnp.add, 0.0, chunk, True)
    br_b = i_b - cum_b
    pm_b = _chunk_scan(br_b, jnp.maximum, neg, chunk, True)
    zero = jnp.zeros_like(cum_f[0:1])
    for h in range(N_H):
        o_ref[0, h] = jnp.concatenate(
            [a[h:h + 1] for a in (cum_f, br_f, pm_f, cum_b, br_b, pm_b)] + [zero, zero], axis=0)


def _gateprep(g_t, b_gates, chunk):
    b, ng, l = g_t.shape
    return pl.pallas_call(
        functools.partial(_gateprep_kernel, chunk=chunk),
        out_shape=jax.ShapeDtypeStruct((b, N_H, N_SCAN_ROWS, l), F32),
        grid=(b,),
        in_specs=[pl.BlockSpec((1, ng, l), lambda bi: (bi, 0, 0)),
                  pl.BlockSpec((ng, 1), lambda bi: (0, 0))],
        out_specs=pl.BlockSpec((1, N_H, N_SCAN_ROWS, l), lambda bi: (bi, 0, 0, 0)),
        compiler_params=_params("parallel"),
        name="gateprep",
    )(g_t, b_gates.reshape(ng, 1))


def _scan_kernel(*refs, chunk, hps, want_h, want_state):
    q_ref, kt_ref, v_ref = refs[:3]
    pos = 3
    if want_h:
        og_ref = refs[pos]
        pos += 1
    rr_ref, c0_ref, m0_ref = refs[pos:pos + 3]
    pos += 3
    if want_h:
        h_ref = refs[pos]
        pos += 1
    if want_state:
        cs_ref, ms_ref = refs[pos:pos + 2]
        pos += 2
    vext_scr = refs[pos]
    if want_h:
        hf_scr, hb_scr = refs[pos + 1:pos + 3]

    l, dh = q_ref.shape[1], q_ref.shape[2] // hps
    nc = l // chunk
    rep = chunk // dh
    jj = lax.broadcasted_iota(jnp.int32, (chunk, chunk), 0)
    ss = lax.broadcasted_iota(jnp.int32, (chunk, chunk), 1)
    causal = (jj >= ss, jj <= ss)

    for hh in range(hps):
        hl = slice(hh * dh, (hh + 1) * dh)
        vext_scr[hh, :, 0:dh] = v_ref[0, :, hl]
        vext_scr[hh, :, dh:2 * dh] = jnp.ones((l, dh), BF16)
        cols = rr_ref[0, hh].T
        cst = [c0_ref[0, d, hh] for d in range(2)]
        mst = [m0_ref[0, d, hh][:, 0:1] for d in range(2)]

        for i in range(nc):
            for d in range(2):
                c = i if d == 0 else nc - 1 - i
                lo = c * chunk
                end = chunk - 1 if d == 0 else 0
                r0 = 3 * d
                qc = q_ref[0, lo:lo + chunk, hl]
                ktc = kt_ref[0, hl, lo:lo + chunk]
                vc = vext_scr[hh, lo:lo + chunk, :]
                cum_c = cols[lo:lo + chunk, r0:r0 + 1]
                pm_c = cols[lo:lo + chunk, r0 + 2:r0 + 3]
                br_r = rr_ref[0, hh, r0 + 1:r0 + 2, lo:lo + chunk]
                tot = rr_ref[0, hh, r0:r0 + 1, lo + end:lo + end + 1]
                pm_end = rr_ref[0, hh, r0 + 2:r0 + 3, lo + end:lo + end + 1]
                m0 = mst[d]
                mm = jnp.broadcast_to(jnp.maximum(m0, pm_c), (chunk, dh))
                d_in = jnp.where(causal[d],
                                 jnp.exp2(br_r - jnp.concatenate([mm] * rep, axis=1)), 0.0)
                d_st = jnp.exp2(m0 - mm)
                s = jnp.dot(qc, ktc, preferred_element_type=F32)
                p = (s * d_in).astype(BF16)
                qd = (qc.astype(F32) * d_st).astype(BF16)
                both = (jnp.dot(p, vc, preferred_element_type=F32)
                        + jnp.dot(qd, cst[d].astype(BF16), preferred_element_type=F32))
                if want_h:
                    m_j = jnp.broadcast_to(cum_c, (chunk, dh)) + mm
                    hc = both[:, 0:dh] / jnp.maximum(jnp.abs(both[:, dh:2 * dh]), jnp.exp2(-m_j))
                    (hf_scr if d == 0 else hb_scr)[hh, lo:lo + chunk, :] = hc
                kw = (ktc.astype(F32) * d_in[end:end + 1, :]).astype(BF16)
                upd = jnp.dot(kw, vc, preferred_element_type=F32)
                dn = d_st[end:end + 1, :]
                cst[d] = jnp.concatenate([dn, dn], axis=1) * cst[d] + upd
                mst[d] = tot + jnp.maximum(m0, pm_end)

        if want_h:
            gate = jax.nn.sigmoid(og_ref[0, :, hl].astype(F32))
            h_ref[0, :, hl] = (gate * (hf_scr[hh] + hb_scr[hh])).astype(h_ref.dtype)
        if want_state:
            for d in range(2):
                cs_ref[0, d, hh] = cst[d]
                ms_ref[0, d, hh] = jnp.broadcast_to(mst[d], (1, LANES))


def _scan(q, kt, v, og, rr, c0, m0, *, hps, want_h, want_state):
    b, l, dm = v.shape
    dh = dm // N_H
    chunk = min(SCAN_CHUNK, l)
    assert chunk % dh == 0 and l % chunk == 0 and N_H % hps == 0
    tok = pl.BlockSpec((1, l, hps * dh), lambda bi, h: (bi, 0, h))
    in_specs = [tok, pl.BlockSpec((1, hps * dh, l), lambda bi, h: (bi, h, 0)), tok]
    args = [q, kt, v]
    if want_h:
        in_specs.append(tok)
        args.append(og)
    st_c = pl.BlockSpec((1, 2, hps, dh, 2 * dh), lambda bi, h: (bi, 0, h, 0, 0))
    st_m = pl.BlockSpec((1, 2, hps, 1, LANES), lambda bi, h: (bi, 0, h, 0, 0))
    in_specs += [pl.BlockSpec((1, hps, N_SCAN_ROWS, l), lambda bi, h: (bi, h, 0, 0)), st_c, st_m]
    args += [rr, c0, m0]
    out_shape, out_specs = [], []
    scratch = [pltpu.VMEM((hps, l, 2 * dh), BF16)]
    if want_h:
        out_shape.append(jax.ShapeDtypeStruct((b, l, dm), BF16))
        out_specs.append(tok)
        scratch += [pltpu.VMEM((hps, l, dh), F32), pltpu.VMEM((hps, l, dh), F32)]
    if want_state:
        out_shape += [jax.ShapeDtypeStruct(c0.shape, F32), jax.ShapeDtypeStruct(m0.shape, F32)]
        out_specs += [st_c, st_m]
    return pl.pallas_call(
        functools.partial(_scan_kernel, chunk=chunk, hps=hps, want_h=want_h, want_state=want_state),
        out_shape=out_shape,
        grid=(b, N_H // hps),
        in_specs=in_specs,
        out_specs=out_specs,
        scratch_shapes=scratch,
        compiler_params=_params("parallel", "parallel"),
        name="scan_h" if want_h else "scan_state",
    )(*args)


def _twiddle(z, k, n):
    zr, zi = z
    k %= n
    if k == 0:
        return z
    if 4 * k == n:
        return zi, -zr
    if 2 * k == n:
        return -zr, -zi
    if 4 * k == 3 * n:
        return -zi, zr
    wr, wi = math.cos(2.0 * math.pi * k / n), -math.sin(2.0 * math.pi * k / n)
    return zr * wr - zi * wi, zr * wi + zi * wr


def _fft_list(xs):
    n = len(xs)
    if n == 1:
        return xs
    ev, od = _fft_list(xs[0::2]), _fft_list(xs[1::2])
    out = [None] * n
    for k in range(n // 2):
        tr, ti = _twiddle(od[k], k, n)
        out[k] = (ev[k][0] + tr, ev[k][1] + ti)
        out[k + n // 2] = (ev[k][0] - tr, ev[k][1] - ti)
    return out


def _fourier_kernel(u_ref, wc_ref, tw_ref, cs_ref, o_ref, zz_scr, r_scr, *, fg, n1):
    l, df = u_ref.shape[1], u_ref.shape[2]
    n2 = l // n1
    for g in range(df // fg):
        r = jnp.dot(u_ref[0, :, g * fg:(g + 1) * fg], wc_ref[...], preferred_element_type=F32)
        xs = [(r[j * n2:(j + 1) * n2, 0:fg], r[j * n2:(j + 1) * n2, fg:2 * fg]) for j in range(n1)]
        zs = _fft_list(xs)
        for k1 in range(n1):
            zr, zi = zs[k1]
            if k1 > 0:
                tc, ts = tw_ref[0, k1], tw_ref[1, k1]
                zr, zi = zr * tc + zi * ts, zi * tc - zr * ts
            zz_scr[k1, 0:n2, g * fg:(g + 1) * fg] = zr.astype(BF16)
            zz_scr[k1, n2:2 * n2, g * fg:(g + 1) * fg] = zi.astype(BF16)
    pitch = r_scr.shape[1] // n1
    for k1 in range(n1):
        res = jnp.dot(cs_ref[...], zz_scr[k1], preferred_element_type=F32)
        for c in range(df // LANES):
            r_scr[c, k1 * pitch:k1 * pitch + n2, :] = res[:, c * LANES:(c + 1) * LANES]

    def emit(j, carry):
        for c in range(df // LANES):
            a = r_scr[c, pl.ds(2 * j, n1, stride=pitch), :]
            b = r_scr[c, pl.ds(2 * j + 1, n1, stride=pitch), :]
            o_ref[0, pl.ds(pl.multiple_of(2 * n1 * j, 2 * n1), 2 * n1), c * LANES:(c + 1) * LANES] = (
                jnp.concatenate([a, b], axis=0).astype(o_ref.dtype))
        return carry

    lax.fori_loop(0, n2 // 2, emit, 0, unroll=8)


def _dft_angles(rows, cols, n):
    ang = ((rows[:, None] * cols[None, :]) % n).astype(F32) * (2.0 * math.pi / n)
    return jnp.cos(ang), jnp.sin(ang)


def _fourier(u):
    b, l, df = u.shape
    fg = df // N_FG
    n1 = 8
    n2 = l // n1
    ar = lambda n: jnp.arange(n, dtype=jnp.int32)
    cc, sc = _dft_angles(ar(fg), ar(fg), fg)
    wc = (jnp.concatenate([cc, -sc], axis=1) * fg ** -0.5).astype(BF16)
    tc, ts = _dft_angles(ar(n1), ar(n2), l)
    tw = jnp.broadcast_to(jnp.stack([tc, ts])[..., None], (2, n1, n2, fg))
    c2, s2 = _dft_angles(ar(n2), ar(n2), n2)
    cs = (jnp.concatenate([c2, s2], axis=1) * l ** -0.5).astype(BF16)
    const = lambda a: pl.BlockSpec(a.shape, lambda bi: (0,) * a.ndim, pipeline_mode=pl.Buffered(1))
    pitch = n2 + 4
    return pl.pallas_call(
        functools.partial(_fourier_kernel, fg=fg, n1=n1),
        out_shape=jax.ShapeDtypeStruct((b, l, df), BF16),
        grid=(b,),
        in_specs=[pl.BlockSpec((1, l, df), lambda bi: (bi, 0, 0)), const(wc), const(tw), const(cs)],
        out_specs=pl.BlockSpec((1, l, df), lambda bi: (bi, 0, 0)),
        scratch_shapes=[pltpu.VMEM((n1, 2 * n2, df), BF16),
                        pltpu.VMEM((df // LANES, n1 * pitch, LANES), F32)],
        compiler_params=_params("parallel"),
        name="fourier",
    )(u, wc, tw, cs)


def _merge_kernel(fu_ref, hm_ref, br_ref, x_ref, mod_ref, wf_ref, wm_ref, wo_ref, g2_ref,
                  wr1_ref, wr2_ref, x1_ref, hx_ref, aff_ref):
    tm, d = x_ref.shape[1], x_ref.shape[2]
    ne = aff_ref.shape[1]
    for r0 in range(0, tm, MERGE_SUBTILE):
        rows = slice(r0, r0 + MERGE_SUBTILE)
        yf = jnp.dot(fu_ref[0, rows, :], wf_ref[...], preferred_element_type=F32)
        ym = jnp.dot(hm_ref[0, rows, :], wm_ref[...], preferred_element_type=F32)
        gf = jax.nn.sigmoid(br_ref[0, rows, 0:d].astype(F32))
        gm = jax.nn.sigmoid(br_ref[0, rows, d:2 * d].astype(F32))
        merged = (gf * yf + gm * ym).astype(BF16)
        o = jnp.dot(merged, wo_ref[...], preferred_element_type=F32)
        x1 = x_ref[0, rows, :] + mod_ref[0, 2:3, :] * o
        x1_ref[0, rows, :] = x1
        h2 = _rms_mod(x1, g2_ref[...], mod_ref[0, 3:4, :], mod_ref[0, 4:5, :])
        hi = h2.astype(BF16)
        hx_ref[0, rows, :] = hi
        lo = (h2 - hi.astype(F32)).astype(BF16)
        r = (jnp.dot(hi, wr1_ref[...], preferred_element_type=F32)
             + jnp.dot(lo, wr2_ref[...], preferred_element_type=F32))
        rt = r.T
        logits = rt[0:ne, :] + rt[ne:2 * ne, :]
        e = jnp.exp(logits - jnp.max(logits, axis=0, keepdims=True))
        aff_ref[0, :, rows] = e / jnp.sum(e, axis=0, keepdims=True)


def _merge(fu, hm, pbr, x, mods, wf, wm, wo, g2, w_router, tm):
    b, l, d = x.shape
    tm = min(tm, l)
    ne = w_router.shape[1]
    whi = w_router.astype(BF16)
    wlo = (w_router - whi.astype(F32)).astype(BF16)
    pad = jnp.zeros((d, LANES - 2 * ne), BF16)
    wr1 = jnp.concatenate([whi, wlo, pad], axis=1)
    wr2 = jnp.concatenate([whi, jnp.zeros((d, ne), BF16), pad], axis=1)
    tok = lambda n: pl.BlockSpec((1, tm, n), lambda bi, i: (bi, i, 0))
    const = lambda a: pl.BlockSpec(a.shape, lambda bi, i: (0, 0), pipeline_mode=pl.Buffered(1))
    return pl.pallas_call(
        _merge_kernel,
        out_shape=[jax.ShapeDtypeStruct((b, l, d), F32),
                   jax.ShapeDtypeStruct((b, l, d), BF16),
                   jax.ShapeDtypeStruct((b, ne, l), F32)],
        grid=(b, l // tm),
        in_specs=[tok(fu.shape[2]), tok(hm.shape[2]), tok(pbr.shape[2]), tok(d),
                  pl.BlockSpec((1, N_MOD, d), lambda bi, i: (bi, 0, 0)),
                  const(wf), const(wm), const(wo),
                  pl.BlockSpec((1, d), lambda bi, i: (0, 0)),
                  const(wr1), const(wr2)],
        out_specs=[tok(d), tok(d), pl.BlockSpec((1, ne, tm), lambda bi, i: (bi, 0, i))],
        compiler_params=_params("parallel", "parallel"),
        name="merge",
    )(fu, hm, pbr, x, mods, wf, wm, wo, g2.reshape(1, d), wr1, wr2)


def _topk_kernel(a_ref, u_ref, slot_ref, before_ref, *, k):
    a = a_ref[...]
    l = a.shape[1]
    total = lambda m: jnp.sum(m, axis=1, keepdims=True)
    ge = lambda bits: jnp.where(a >= pltpu.bitcast(bits, F32), 1.0, 0.0)

    def body(i, t):
        cand = t | lax.shift_left(jnp.int32(1), 30 - i)
        return jnp.where(total(ge(cand)) >= k, cand, t)

    t = lax.fori_loop(0, 31, body, jnp.zeros((a.shape[0], 1), jnp.int32))
    chosen = ge(t + 1)
    rem = ge(t) - chosen
    need = k - total(chosen)
    idx = lax.broadcasted_iota(jnp.int32, a.shape, 1).astype(F32)
    for _ in range(N_TIE_EXTRACT):
        cur = jnp.where(rem > 0.0, a, -1.0)
        is_mx = jnp.where(cur == jnp.max(cur, axis=1, keepdims=True), rem, 0.0)
        first = jnp.min(jnp.where(is_mx > 0.0, idx, float(l)), axis=1, keepdims=True)
        pick = jnp.where(idx == first, jnp.where(need > 0.0, 1.0, 0.0), 0.0)
        chosen = chosen + pick
        rem = rem - pick
        need = need - total(pick)
    before = jnp.dot(rem.astype(BF16), u_ref[...], preferred_element_type=F32)
    sel = chosen + jnp.where(before < need, rem, 0.0)
    rank = jnp.dot(sel.astype(BF16), u_ref[...], preferred_element_type=F32)
    slot_ref[...] = jnp.where(sel > 0.0, rank, -1.0)
    before_ref[...] = rank


def _topk(aff_t, k):
    r, l = aff_t.shape
    tr = min(128, r)
    idx = jnp.arange(l, dtype=jnp.int32)
    upper = (idx[:, None] < idx[None, :]).astype(BF16)
    return pl.pallas_call(
        functools.partial(_topk_kernel, k=k),
        out_shape=[jax.ShapeDtypeStruct((r, l), F32)] * 2,
        grid=(r // tr,),
        in_specs=[pl.BlockSpec((tr, l), lambda i: (i, 0)),
                  pl.BlockSpec((l, l), lambda i: (0, 0), pipeline_mode=pl.Buffered(1))],
        out_specs=[pl.BlockSpec((tr, l), lambda i: (i, 0))] * 2,
        compiler_params=_params("parallel"),
        name="topk",
    )(aff_t, upper)


def _moe_kernel(cnt_ref, hx_ref, srow_ref, arow_ref, slot_ref, wg_ref, wu_ref, wd_ref, x1_ref,
                mod_ref, g_ref, xp_ref, o_ref, y_scr, xs_scr, *, cap, ne, nt):
    bi = pl.program_id(0)
    s = pl.program_id(1)
    ng = hx_ref.shape[0]
    win = MOE_WINDOW
    per_sample = ne * (nt + 1)

    @pl.when(s == 0)
    def _():
        for gi in range(ng):
            y_scr[gi, ne * cap:ne * cap + win, :] = jnp.zeros((win, y_scr.shape[2]), BF16)

    @pl.when(s < ne)
    def _():
        l = hx_ref.shape[1]
        tb, gw = MOE_TILE, MOE_GATHER_WINDOW
        xs_scr[...] = jnp.zeros(xs_scr.shape, F32)
        row_i = lax.broadcasted_iota(jnp.int32, (gw, tb), 0)
        lows, n_pass = [], 0
        for gi in range(ng):
            base = (bi * ng + gi) * per_sample + s * (nt + 1)
            for j in range(nt):
                low = (cnt_ref[base + j] // ROW_ALIGN) * ROW_ALIGN
                lows.append(low)
                n_pass = jnp.maximum(n_pass, (cnt_ref[base + j + 1] - low + gw - 1) // gw)

        def gather_pass(w, carry):
            for gi in range(ng):
                for j in range(nt):
                    lo = pl.multiple_of(jnp.minimum(lows[gi * nt + j] + w * gw, cap), ROW_ALIGN)
                    srow_blk = srow_ref[gi, 0, :, j * tb:(j + 1) * tb]
                    p = jnp.where(srow_blk == (row_i + lo).astype(F32), 1.0, 0.0).astype(BF16)
                    xs_scr[gi, pl.ds(lo, gw), :] += jnp.dot(
                        p, hx_ref[gi, j * tb:(j + 1) * tb, :], preferred_element_type=F32)
            return carry

        gather_pass(0, 0)
        lax.fori_loop(1, n_pass, gather_pass, 0)
        xs = jnp.concatenate([xs_scr[gi, 0:cap, :] for gi in range(ng)], axis=0).astype(BF16)
        gate = jnp.dot(xs, wg_ref[0], preferred_element_type=F32)
        up = jnp.dot(xs, wu_ref[0], preferred_element_type=F32)
        hid = (_silu(gate) * up).astype(BF16)
        y = jnp.dot(hid, wd_ref[0], preferred_element_type=F32)
        slot_i = lax.broadcasted_iota(jnp.int32, (cap, l), 0).astype(F32)
        for gi in range(ng):
            hit = srow_ref[gi, 0] == slot_i
            vals = jnp.sum(jnp.where(hit, arow_ref[gi, 0], 0.0), axis=1, keepdims=True)
            y_scr[gi, pl.ds(pl.multiple_of(s * cap, cap), cap), :] = (
                y[gi * cap:(gi + 1) * cap, :] * vals).astype(BF16)

    @pl.when(s >= ne)
    def _():
        tt, d = MOE_TILE, x1_ref.shape[2]
        tps = x1_ref.shape[1] // tt
        gi = (s - ne) // (nt // tps)
        j0 = ((s - ne) % (nt // tps)) * tps
        lane = lax.broadcasted_iota(jnp.int32, (1, ne * win), 1)
        grp, off = lane // win, lane % win
        slots = slot_ref[0].T.astype(BF16)
        tiles, n_pass = [], 0
        for u in range(tps):
            base = (bi * ng + gi) * per_sample + j0 + u
            lows = []
            for e in range(ne):
                start = cnt_ref[base + e * (nt + 1)]
                end = cnt_ref[base + e * (nt + 1) + 1]
                low = (start // SLOT_ALIGN) * SLOT_ALIGN
                lows.append(low)
                n_pass = jnp.maximum(n_pass, (end - low + win - 1) // win)
            sexp = jnp.dot(slots[u * tt:(u + 1) * tt, :], xp_ref[...], preferred_element_type=F32)
            tiles.append((lows, sexp))

        def one_pass(w, accs):
            out = []
            for (lows, sexp), acc in zip(tiles, accs):
                tgt, rows = off, []
                for e in range(ne):
                    lo = jnp.minimum(lows[e] + w * win, cap)
                    tgt = jnp.where(grp == e, off + lo, tgt)
                    rows.append(
                        y_scr[gi, pl.ds(pl.multiple_of(e * cap + lo, SLOT_ALIGN), win), :])
                pt = jnp.where(sexp == tgt.astype(F32), 1.0, 0.0).astype(BF16)
                out.append(acc + jnp.dot(pt, jnp.concatenate(rows, axis=0),
                                         preferred_element_type=F32))
            return tuple(out)

        accs = one_pass(0, tuple(jnp.zeros((tt, d), F32) for _ in range(tps)))
        accs = lax.fori_loop(1, n_pass, one_pass, accs)
        for u, moe in enumerate(accs):
            v = x1_ref[0, u * tt:(u + 1) * tt, :] + mod_ref[gi, 5:6, :] * moe
            ms = jnp.mean(v * v, axis=-1, keepdims=True)
            o_ref[0, u * tt:(u + 1) * tt, :] = v * lax.rsqrt(ms + EPS) * g_ref[...]


def _moe(hx, slot_t, before_t, aff_t, wg, wu, wd, x1, mods, g_final, cap):
    b, l, d = hx.shape
    ne, _, f = wg.shape
    tt = MOE_TILE
    nt = l // tt
    win = MOE_WINDOW
    ng = MOE_GROUP if b % MOE_GROUP == 0 else 1
    assert cap % SLOT_ALIGN == 0 and win % SLOT_ALIGN == 0 and (win & (win - 1)) == 0
    slot_row = slot_t.reshape(b, ne, 1, l)
    aff_row = aff_t.reshape(b, ne, 1, l)
    cnt = jnp.concatenate([before_t[:, :, ::tt], jnp.full((b, ne, 1), cap, F32)], axis=2)
    cnt = cnt.astype(jnp.int32).reshape(b * ne * (nt + 1))
    lane = jnp.arange(ne * win, dtype=jnp.int32)
    expand = (lane[None, :] // win == jnp.arange(ne, dtype=jnp.int32)[:, None]).astype(BF16)
    ex = lambda s: jnp.minimum(s, ne - 1)
    tps = MOE_TILES_PER_STEP if nt % MOE_TILES_PER_STEP == 0 else 1
    ts, spt = tps * tt, nt // tps
    sample = lambda bi, s: bi * ng + jnp.maximum(s - ne, 0) // spt
    tile = lambda s: jnp.maximum(s - ne, 0) % spt
    row = pl.BlockSpec((ng, 1, 1, l), lambda bi, s, c: (bi, ex(s), 0, 0))
    wspec = lambda k, n: pl.BlockSpec((1, k, n), lambda bi, s, c: (ex(s), 0, 0))
    grid_spec = pltpu.PrefetchScalarGridSpec(
        num_scalar_prefetch=1,
        grid=(b // ng, ne + ng * spt),
        in_specs=[pl.BlockSpec((ng, l, d), lambda bi, s, c: (bi, 0, 0),
                               pipeline_mode=pl.Buffered(1)),
                  row, row,
                  pl.BlockSpec((1, ne, ts), lambda bi, s, c: (sample(bi, s), 0, tile(s))),
                  wspec(d, f), wspec(d, f), wspec(f, d),
                  pl.BlockSpec((1, ts, d), lambda bi, s, c: (sample(bi, s), tile(s), 0)),
                  pl.BlockSpec((ng, N_MOD, d), lambda bi, s, c: (bi, 0, 0)),
                  pl.BlockSpec((1, d), lambda bi, s, c: (0, 0)),
                  pl.BlockSpec((ne, ne * win), lambda bi, s, c: (0, 0))],
        out_specs=pl.BlockSpec((1, ts, d), lambda bi, s, c: (sample(bi, s), tile(s), 0)),
        scratch_shapes=[pltpu.VMEM((ng, ne * cap + win, d), BF16),
                        pltpu.VMEM((ng, cap + MOE_GATHER_WINDOW, d), F32)])
    return pl.pallas_call(
        functools.partial(_moe_kernel, cap=cap, ne=ne, nt=nt),
        out_shape=jax.ShapeDtypeStruct((b, l, d), F32),
        grid_spec=grid_spec,
        compiler_params=_params("parallel", "arbitrary"),
        name="moe",
    )(cnt, hx, slot_row, aff_row, slot_t, wg, wu, wd, x1, mods, g_final.reshape(1, d), expand)


def kernel(x, c, ctx, c_ctx, w_ada, b_ada, g_norm1, w_in, b_gates, w_conv, w_fourier, w_mlstm,
           w_out, g_norm2, w_router, w_gate_e, w_up_e, w_down_e, g_final):
    assert w_ada.shape[0] == 1, "single-layer stack"
    b, l, d = x.shape
    lc = ctx.shape[1]
    d_f = d // 2
    d_m = d // 2
    dh = d_m // N_H
    ne = w_router.shape[2]
    cap = CAP_FACTOR * l // ne
    off_qk = d_f
    off_v = off_qk + 2 * d_m
    off_o = off_v + d_m
    off_g = off_o + d_m
    off_br = off_g + N_GATE_KINDS * N_H

    n_c = b + 1
    n_pad = -n_c % 8
    c_all = jnp.concatenate([c, c_ctx[None, :], jnp.zeros((n_pad, d), F32)], axis=0)
    mods = _ada(c_all, w_ada[0], b_ada[0]).reshape(n_c + n_pad, N_MOD, d)

    w = w_in[0]
    wb = w.astype(BF16)
    w_f, w_qk, w_v, w_o, w_br = (wb[:, 0:off_qk], wb[:, off_qk:off_v], wb[:, off_v:off_o],
                                 wb[:, off_o:off_g], wb[:, off_br:])
    w_gt = wb[:, off_g:off_br].T
    w9 = w_conv[0].reshape(9, 2 * d_m)
    qk_scale = jnp.concatenate([jnp.ones((1, d_m), F32), jnp.full((1, d_m), dh ** -0.5, F32)], axis=1)

    pc_qk, pc_v, gc_t = _inproj(ctx, mods, lambda bi, i: (b, 0, 0), g_norm1[0],
                                [w_qk, w_v], w_gt, tm=lc)
    q_c = _conv(pc_qk, w9, qk_scale, 1, 0, d_m, False)
    kt_c = _conv(pc_qk, w9, qk_scale, 1, d_m, 2 * d_m, True)
    rr_c = _gateprep(gc_t, b_gates[0], min(SCAN_CHUNK, lc))
    c0 = jnp.zeros((b, 2, N_H, dh, 2 * dh), F32)
    m0 = jnp.zeros((b, 2, N_H, 1, LANES), F32)
    c_seed, m_seed = _scan(q_c, kt_c, pc_v, None, rr_c, c0, m0, hps=N_H, want_h=False,
                           want_state=True)

    p_f, p_qk, p_v, p_o, p_br, g_t = _inproj(x, mods, lambda bi, i: (bi, 0, 0), g_norm1[0],
                                             [w_f, w_qk, w_v, w_o, w_br], w_gt, tm=1024)
    q_x = _conv(p_qk, w9, qk_scale, l // GRID_W, 0, d_m, False)
    kt_x = _conv(p_qk, w9, qk_scale, l // GRID_W, d_m, 2 * d_m, True)
    rr_x = _gateprep(g_t, b_gates[0], SCAN_CHUNK)
    (hm,) = _scan(q_x, kt_x, p_v, p_o, rr_x, c_seed, m_seed, hps=SCAN_HEADS_PER_STEP, want_h=True,
                  want_state=False)
    fu = _fourier(p_f)
    x1, hx2, aff_t = _merge(fu, hm, p_br, x, mods, w_fourier[0].astype(BF16),
                            w_mlstm[0].astype(BF16), w_out[0].astype(BF16), g_norm2[0], w_router[0],
                            tm=1024)

    slot_t, before_t = _topk(aff_t.reshape(b * ne, l), cap)
    return _moe(hx2, slot_t.reshape(b, ne, l), before_t.reshape(b, ne, l), aff_t,
                w_gate_e[0].astype(BF16), w_up_e[0].astype(BF16), w_down_e[0].astype(BF16),
                x1, mods, g_final, cap)
```

```python
import functools
import math

import jax
import jax.numpy as jnp
from jax import lax
from jax.experimental import pallas as pl
from jax.experimental.pallas import tpu as pltpu

F32 = jnp.float32
BF16 = jnp.bfloat16

GRID_W = 64
N_FG = 4
N_H = 4
N_GATE_KINDS = 4
N_EXPERTS = 16
CAP_FACTOR = 2
EPS = 1e-6
LOG2E = math.log2(math.e)
N_MOD = 6

LANES = 128
V7X_VMEM_BYTES = 64 * 1024 * 1024
VMEM_LIMIT = V7X_VMEM_BYTES - 8 * 1024 * 1024

SCAN_CHUNK = 256
SCAN_HEADS_PER_STEP = 1
N_SCAN_ROWS = 8
N_TIE_EXTRACT = 3
MERGE_SUBTILE = 512
MOE_GROUP = 2
MOE_TILE = 256
MOE_TILES_PER_STEP = 2
MOE_WINDOW = 64
MOE_GATHER_WINDOW = 128
ROW_ALIGN = 8
SLOT_ALIGN = 16


def _params(*sem):
    return pltpu.CompilerParams(dimension_semantics=sem, vmem_limit_bytes=VMEM_LIMIT)


def _silu(v):
    return v * jax.nn.sigmoid(v)


def _rms_mod(v, g, shift, scale):
    ms = jnp.mean(v * v, axis=-1, keepdims=True)
    return (v * lax.rsqrt(ms + EPS) * g) * (1.0 + scale) + shift


def _ada_kernel(c_ref, w_ref, b_ref, o_ref):
    s = _silu(c_ref[...])
    o_ref[...] = jnp.dot(s, w_ref[...], preferred_element_type=F32,
                         precision=lax.Precision.HIGHEST) + b_ref[...]


def _ada(c_all, w_ada, b_ada):
    n, d = c_all.shape
    cols = w_ada.shape[1]
    tn = 512
    return pl.pallas_call(
        _ada_kernel,
        out_shape=jax.ShapeDtypeStruct((n, cols), F32),
        grid=(cols // tn,),
        in_specs=[pl.BlockSpec((n, d), lambda j: (0, 0)),
                  pl.BlockSpec((d, tn), lambda j: (0, j)),
                  pl.BlockSpec((1, tn), lambda j: (0, j))],
        out_specs=pl.BlockSpec((n, tn), lambda j: (0, j)),
        compiler_params=_params("parallel"),
        name="ada",
    )(c_all, w_ada, b_ada.reshape(1, cols))


def _inproj_kernel(x_ref, mod_ref, g_ref, *refs, n_seg):
    w_refs = refs[:n_seg]
    wgt_ref = refs[n_seg]
    o_refs = refs[n_seg + 1:2 * n_seg + 1]
    gt_ref = refs[2 * n_seg + 1]
    h = _rms_mod(x_ref[0], g_ref[...], mod_ref[0, 0:1, :], mod_ref[0, 1:2, :])
    hb = h.astype(BF16)
    for w_ref, o_ref in zip(w_refs, o_refs):
        o_ref[0] = jnp.dot(hb, w_ref[...], preferred_element_type=F32).astype(o_ref.dtype)
    gt_ref[0] = lax.dot_general(wgt_ref[...], hb, (((1,), (1,)), ((), ())),
                                preferred_element_type=F32)


def _inproj(x, mods, mod_row, g, w_segs, w_gt, tm):
    b, l, d = x.shape
    tm = min(tm, l)
    n_seg = len(w_segs)
    ng = w_gt.shape[0]
    const2 = lambda bi, i: (0, 0)
    in_specs = [pl.BlockSpec((1, tm, d), lambda bi, i: (bi, i, 0)),
                pl.BlockSpec((1, N_MOD, d), mod_row),
                pl.BlockSpec((1, d), const2)]
    in_specs += [pl.BlockSpec(w.shape, const2, pipeline_mode=pl.Buffered(1)) for w in w_segs]
    in_specs += [pl.BlockSpec(w_gt.shape, const2)]
    out_shape = [jax.ShapeDtypeStruct((b, l, w.shape[1]), BF16) for w in w_segs]
    out_shape += [jax.ShapeDtypeStruct((b, ng, l), F32)]
    out_specs = [pl.BlockSpec((1, tm, w.shape[1]), lambda bi, i: (bi, i, 0)) for w in w_segs]
    out_specs += [pl.BlockSpec((1, ng, tm), lambda bi, i: (bi, 0, i))]
    return pl.pallas_call(
        functools.partial(_inproj_kernel, n_seg=n_seg),
        out_shape=out_shape,
        grid=(b, l // tm),
        in_specs=in_specs,
        out_specs=out_specs,
        compiler_params=_params("parallel", "parallel"),
        name="inproj",
    )(x, mods, g.reshape(1, d), *w_segs, w_gt)


def _conv_kernel(p_ref, w_ref, s_ref, o_ref, *, rows, width, transpose):
    v = p_ref[0].astype(F32)
    l, ct = v.shape
    grid = lambda a: a.reshape(rows, width, ct)
    v3, vl3, vr3 = grid(v), grid(pltpu.roll(v, 1, 0)), grid(pltpu.roll(v, l - 1, 0))
    col = lax.broadcasted_iota(jnp.int32, (1, width, ct), 1)
    w = w_ref[...]
    wl = [jnp.where(col == 0, 0.0, w[3 * dr:3 * dr + 1, :][None]) for dr in range(3)]
    wr = [jnp.where(col == width - 1, 0.0, w[3 * dr + 2:3 * dr + 3, :][None]) for dr in range(3)]

    def tap(dr):
        return vl3 * wl[dr] + v3 * w[3 * dr + 1:3 * dr + 2, :][None] + vr3 * wr[dr]

    acc = tap(1)
    if rows > 1:
        edge = jnp.zeros((1, width, ct), F32)
        acc = acc + jnp.concatenate([edge, tap(0)[:-1]], axis=0)
        acc = acc + jnp.concatenate([tap(2)[1:], edge], axis=0)
    acc = acc.reshape(l, ct)
    y = _silu(acc) * s_ref[...]
    o_ref[0] = (y.T if transpose else y).astype(o_ref.dtype)


def _conv(p, w9, scale, rows, col_lo, col_hi, transpose):
    b, l, _ = p.shape
    ct = 512
    j0 = col_lo // ct
    n = col_hi - col_lo
    if transpose:
        out_shape = jax.ShapeDtypeStruct((b, n, l), BF16)
        out_spec = pl.BlockSpec((1, ct, l), lambda bi, j: (bi, j, 0))
    else:
        out_shape = jax.ShapeDtypeStruct((b, l, n), BF16)
        out_spec = pl.BlockSpec((1, l, ct), lambda bi, j: (bi, 0, j))
    return pl.pallas_call(
        functools.partial(_conv_kernel, rows=rows, width=l // rows, transpose=transpose),
        out_shape=out_shape,
        grid=(b, n // ct),
        in_specs=[pl.BlockSpec((1, l, ct), lambda bi, j: (bi, 0, j + j0)),
                  pl.BlockSpec((9, ct), lambda bi, j: (0, j + j0)),
                  pl.BlockSpec((1, ct), lambda bi, j: (0, j + j0))],
        out_specs=out_spec,
        compiler_params=_params("parallel", "parallel"),
        name="conv_kt" if transpose else "conv_q",
    )(p, w9, scale)


def _log_sigmoid(v):
    return jnp.minimum(v, 0.0) - jnp.log1p(jnp.exp(-jnp.abs(v)))


def _chunk_scan(v, op, ident, chunk, reverse):
    l = v.shape[1]
    pos = lax.broadcasted_iota(jnp.int32, v.shape, 1) % chunk
    s = 1
    while s < chunk:
        if reverse:
            shifted = jnp.where(pos < chunk - s, pltpu.roll(v, l - s, 1), ident)
        else:
            shifted = jnp.where(pos >= s, pltpu.roll(v, s, 1), ident)
        v = op(v, shifted)
        s *= 2
    return v


def _gateprep_kernel(g_ref, b_ref, o_ref, *, chunk):
    g = g_ref[0] + b_ref[...]
    i_f = g[0:N_H] * LOG2E
    f_f = _log_sigmoid(g[N_H:2 * N_H]) * LOG2E
    i_b = g[2 * N_H:3 * N_H] * LOG2E
    f_b = _log_sigmoid(g[3 * N_H:4 * N_H]) * LOG2E
    neg = -jnp.inf
    cum_f = _chunk_scan(f_f, jnp.add, 0.0, chunk, False)
    br_f = i_f - cum_f
    pm_f = _chunk_scan(br_f, jnp.maximum, neg, chunk, False)
    cum_b = _chunk_scan(f_b, jnp.add, 0.0, chunk, True)
    br_b = i_b - cum_b
    pm_b = _chunk_scan(br_b, jnp.maximum, neg, chunk, True)
    zero = jnp.zeros_like(cum_f[0:1])
    for h in range(N_H):
        o_ref[0, h] = jnp.concatenate(
            [a[h:h + 1] for a in (cum_f, br_f, pm_f, cum_b, br_b, pm_b)] + [zero, zero], axis=0)


def _gateprep(g_t, b_gates, chunk):
    b, ng, l = g_t.shape
    return pl.pallas_call(
        functools.partial(_gateprep_kernel, chunk=chunk),
        out_shape=jax.ShapeDtypeStruct((b, N_H, N_SCAN_ROWS, l), F32),
        grid=(b,),
        in_specs=[pl.BlockSpec((1, ng, l), lambda bi: (bi, 0, 0)),
                  pl.BlockSpec((ng, 1), lambda bi: (0, 0))],
        out_specs=pl.BlockSpec((1, N_H, N_SCAN_ROWS, l), lambda bi: (bi, 0, 0, 0)),
        compiler_params=_params("parallel"),
        name="gateprep",
    )(g_t, b_gates.reshape(ng, 1))


def _scan_kernel(*refs, chunk, hps, want_h, want_state):
    q_ref, kt_ref, v_ref, rr_ref, c0_ref, m0_ref = refs[:6]
    pos = 6
    if want_h:
        h_ref = refs[pos]
        pos += 1
    if want_state:
        cs_ref, ms_ref = refs[pos:pos + 2]
        pos += 2
    vext_scr = refs[pos]
    if want_h:
        hf_scr, hb_scr = refs[pos + 1:pos + 3]

    l, dh = q_ref.shape[1], q_ref.shape[2] // hps
    nc = l // chunk
    rep = chunk // dh
    jj = lax.broadcasted_iota(jnp.int32, (chunk, chunk), 0)
    ss = lax.broadcasted_iota(jnp.int32, (chunk, chunk), 1)
    causal = (jj >= ss, jj <= ss)

    for hh in range(hps):
        hl = slice(hh * dh, (hh + 1) * dh)
        vext_scr[hh, :, 0:dh] = v_ref[0, :, hl]
        vext_scr[hh, :, dh:2 * dh] = jnp.ones((l, dh), BF16)
        cols = rr_ref[0, hh].T
        cst = [c0_ref[0, d, hh] for d in range(2)]
        mst = [m0_ref[0, d, hh][:, 0:1] for d in range(2)]

        for i in range(nc):
            for d in range(2):
                c = i if d == 0 else nc - 1 - i
                lo = c * chunk
                end = chunk - 1 if d == 0 else 0
                r0 = 3 * d
                qc = q_ref[0, lo:lo + chunk, hl]
                ktc = kt_ref[0, hl, lo:lo + chunk]
                vc = vext_scr[hh, lo:lo + chunk, :]
                cum_c = cols[lo:lo + chunk, r0:r0 + 1]
                pm_c = cols[lo:lo + chunk, r0 + 2:r0 + 3]
                br_r = rr_ref[0, hh, r0 + 1:r0 + 2, lo:lo + chunk]
                tot = rr_ref[0, hh, r0:r0 + 1, lo + end:lo + end + 1]
                pm_end = rr_ref[0, hh, r0 + 2:r0 + 3, lo + end:lo + end + 1]
                m0 = mst[d]
                mm = jnp.broadcast_to(jnp.maximum(m0, pm_c), (chunk, dh))
                d_in = jnp.where(causal[d],
                                 jnp.exp2(br_r - jnp.concatenate([mm] * rep, axis=1)), 0.0)
                d_st = jnp.exp2(m0 - mm)
                s = jnp.dot(qc, ktc, preferred_element_type=F32)
                p = (s * d_in).astype(BF16)
                qd = (qc.astype(F32) * d_st).astype(BF16)
                both = (jnp.dot(p, vc, preferred_element_type=F32)
                        + jnp.dot(qd, cst[d].astype(BF16), preferred_element_type=F32))
                if want_h:
                    m_j = jnp.broadcast_to(cum_c, (chunk, dh)) + mm
                    hc = both[:, 0:dh] / jnp.maximum(jnp.abs(both[:, dh:2 * dh]), jnp.exp2(-m_j))
                    (hf_scr if d == 0 else hb_scr)[hh, lo:lo + chunk, :] = hc
                kw = (ktc.astype(F32) * d_in[end:end + 1, :]).astype(BF16)
                upd = jnp.dot(kw, vc, preferred_element_type=F32)
                dn = d_st[end:end + 1, :]
                cst[d] = jnp.concatenate([dn, dn], axis=1) * cst[d] + upd
                mst[d] = tot + jnp.maximum(m0, pm_end)

        if want_h:
            h_ref[0, :, hl] = (hf_scr[hh] + hb_scr[hh]).astype(h_ref.dtype)
        if want_state:
            for d in range(2):
                cs_ref[0, d, hh] = cst[d]
                ms_ref[0, d, hh] = jnp.broadcast_to(mst[d], (1, LANES))


def _scan(q, kt, v, rr, c0, m0, *, hps, want_h, want_state):
    b, l, dm = v.shape
    dh = dm // N_H
    chunk = min(SCAN_CHUNK, l)
    assert chunk % dh == 0 and l % chunk == 0 and N_H % hps == 0
    tok = pl.BlockSpec((1, l, hps * dh), lambda bi, h: (bi, 0, h))
    in_specs = [tok, pl.BlockSpec((1, hps * dh, l), lambda bi, h: (bi, h, 0)), tok]
    args = [q, kt, v]
    st_c = pl.BlockSpec((1, 2, hps, dh, 2 * dh), lambda bi, h: (bi, 0, h, 0, 0))
    st_m = pl.BlockSpec((1, 2, hps, 1, LANES), lambda bi, h: (bi, 0, h, 0, 0))
    in_specs += [pl.BlockSpec((1, hps, N_SCAN_ROWS, l), lambda bi, h: (bi, h, 0, 0)), st_c, st_m]
    args += [rr, c0, m0]
    out_shape, out_specs = [], []
    scratch = [pltpu.VMEM((hps, l, 2 * dh), BF16)]
    if want_h:
        out_shape.append(jax.ShapeDtypeStruct((b, l, dm), BF16))
        out_specs.append(tok)
        scratch += [pltpu.VMEM((hps, l, dh), F32), pltpu.VMEM((hps, l, dh), F32)]
    if want_state:
        out_shape += [jax.ShapeDtypeStruct(c0.shape, F32), jax.ShapeDtypeStruct(m0.shape, F32)]
        out_specs += [st_c, st_m]
    return pl.pallas_call(
        functools.partial(_scan_kernel, chunk=chunk, hps=hps, want_h=want_h, want_state=want_state),
        out_shape=out_shape,
        grid=(b, N_H // hps),
        in_specs=in_specs,
        out_specs=out_specs,
        scratch_shapes=scratch,
        compiler_params=_params("parallel", "parallel"),
        name="scan_h" if want_h else "scan_state",
    )(*args)


def _twiddle(z, k, n):
    zr, zi = z
    k %= n
    if k == 0:
        return z
    if 4 * k == n:
        return zi, -zr
    if 2 * k == n:
        return -zr, -zi
    if 4 * k == 3 * n:
        return -zi, zr
    wr, wi = math.cos(2.0 * math.pi * k / n), -math.sin(2.0 * math.pi * k / n)
    return zr * wr - zi * wi, zr * wi + zi * wr


def _fft_list(xs):
    n = len(xs)
    if n == 1:
        return xs
    ev, od = _fft_list(xs[0::2]), _fft_list(xs[1::2])
    out = [None] * n
    for k in range(n // 2):
        tr, ti = _twiddle(od[k], k, n)
        out[k] = (ev[k][0] + tr, ev[k][1] + ti)
        out[k + n // 2] = (ev[k][0] - tr, ev[k][1] - ti)
    return out


def _fourier_kernel(u_ref, wc_ref, tw_ref, cs_ref, o_ref, zz_scr, r_scr, *, fg, n1):
    l, df = u_ref.shape[1], u_ref.shape[2]
    n2 = l // n1
    for g in range(df // fg):
        r = jnp.dot(u_ref[0, :, g * fg:(g + 1) * fg], wc_ref[...], preferred_element_type=F32)
        xs = [(r[j * n2:(j + 1) * n2, 0:fg], r[j * n2:(j + 1) * n2, fg:2 * fg]) for j in range(n1)]
        zs = _fft_list(xs)
        for k1 in range(n1):
            zr, zi = zs[k1]
            if k1 > 0:
                tc, ts = tw_ref[0, k1], tw_ref[1, k1]
                zr, zi = zr * tc + zi * ts, zi * tc - zr * ts
            zz_scr[k1, 0:n2, g * fg:(g + 1) * fg] = zr.astype(BF16)
            zz_scr[k1, n2:2 * n2, g * fg:(g + 1) * fg] = zi.astype(BF16)
    pitch = r_scr.shape[1] // n1
    for k1 in range(n1):
        res = jnp.dot(cs_ref[...], zz_scr[k1], preferred_element_type=F32)
        for c in range(df // LANES):
            r_scr[c, k1 * pitch:k1 * pitch + n2, :] = res[:, c * LANES:(c + 1) * LANES]

    def emit(j, carry):
        for c in range(df // LANES):
            a = r_scr[c, pl.ds(2 * j, n1, stride=pitch), :]
            b = r_scr[c, pl.ds(2 * j + 1, n1, stride=pitch), :]
            o_ref[0, pl.ds(pl.multiple_of(2 * n1 * j, 2 * n1), 2 * n1), c * LANES:(c + 1) * LANES] = (
                jnp.concatenate([a, b], axis=0).astype(o_ref.dtype))
        return carry

    lax.fori_loop(0, n2 // 2, emit, 0, unroll=8)


def _dft_angles(rows, cols, n):
    ang = ((rows[:, None] * cols[None, :]) % n).astype(F32) * (2.0 * math.pi / n)
    return jnp.cos(ang), jnp.sin(ang)


def _fourier(u):
    b, l, df = u.shape
    fg = df // N_FG
    n1 = 8
    n2 = l // n1
    ar = lambda n: jnp.arange(n, dtype=jnp.int32)
    cc, sc = _dft_angles(ar(fg), ar(fg), fg)
    wc = (jnp.concatenate([cc, -sc], axis=1) * fg ** -0.5).astype(BF16)
    tc, ts = _dft_angles(ar(n1), ar(n2), l)
    tw = jnp.broadcast_to(jnp.stack([tc, ts])[..., None], (2, n1, n2, fg))
    c2, s2 = _dft_angles(ar(n2), ar(n2), n2)
    cs = (jnp.concatenate([c2, s2], axis=1) * l ** -0.5).astype(BF16)
    const = lambda a: pl.BlockSpec(a.shape, lambda bi: (0,) * a.ndim, pipeline_mode=pl.Buffered(1))
    pitch = n2 + 4
    return pl.pallas_call(
        functools.partial(_fourier_kernel, fg=fg, n1=n1),
        out_shape=jax.ShapeDtypeStruct((b, l, df), BF16),
        grid=(b,),
        in_specs=[pl.BlockSpec((1, l, df), lambda bi: (bi, 0, 0)), const(wc), const(tw), const(cs)],
        out_specs=pl.BlockSpec((1, l, df), lambda bi: (bi, 0, 0)),
        scratch_shapes=[pltpu.VMEM((n1, 2 * n2, df), BF16),
                        pltpu.VMEM((df // LANES, n1 * pitch, LANES), F32)],
        compiler_params=_params("parallel"),
        name="fourier",
    )(u, wc, tw, cs)


def _merge_kernel(fu_ref, hm_ref, x_ref, mod_ref, g1_ref, wbr_ref, wog_ref, wf_ref, wm_ref, wo_ref,
                  g2_ref, wr1_ref, wr2_ref, x1_ref, hx_ref, aff_ref):
    tm, d = x_ref.shape[1], x_ref.shape[2]
    ne = aff_ref.shape[1]
    for r0 in range(0, tm, MERGE_SUBTILE):
        rows = slice(r0, r0 + MERGE_SUBTILE)
        x = x_ref[0, rows, :]
        h1 = _rms_mod(x, g1_ref[...], mod_ref[0, 0:1, :], mod_ref[0, 1:2, :]).astype(BF16)
        br = jnp.dot(h1, wbr_ref[...], preferred_element_type=F32)
        og = jnp.dot(h1, wog_ref[...], preferred_element_type=F32)
        yf = jnp.dot(fu_ref[0, rows, :], wf_ref[...], preferred_element_type=F32)
        hm = (jax.nn.sigmoid(og) * hm_ref[0, rows, :].astype(F32)).astype(BF16)
        ym = jnp.dot(hm, wm_ref[...], preferred_element_type=F32)
        merged = (jax.nn.sigmoid(br[:, 0:d]) * yf + jax.nn.sigmoid(br[:, d:2 * d]) * ym).astype(BF16)
        o = jnp.dot(merged, wo_ref[...], preferred_element_type=F32)
        x1 = x + mod_ref[0, 2:3, :] * o
        x1_ref[0, rows, :] = x1
        h2 = _rms_mod(x1, g2_ref[...], mod_ref[0, 3:4, :], mod_ref[0, 4:5, :])
        hi = h2.astype(BF16)
        hx_ref[0, rows, :] = hi
        lo = (h2 - hi.astype(F32)).astype(BF16)
        r = (jnp.dot(hi, wr1_ref[...], preferred_element_type=F32)
             + jnp.dot(lo, wr2_ref[...], preferred_element_type=F32))
        rt = r.T
        logits = rt[0:ne, :] + rt[ne:2 * ne, :]
        e = jnp.exp(logits - jnp.max(logits, axis=0, keepdims=True))
        aff_ref[0, :, rows] = e / jnp.sum(e, axis=0, keepdims=True)


def _merge(fu, hm, x, mods, g1, w_br, w_og, wf, wm, wo, g2, w_router, tm):
    b, l, d = x.shape
    tm = min(tm, l)
    ne = w_router.shape[1]
    whi = w_router.astype(BF16)
    wlo = (w_router - whi.astype(F32)).astype(BF16)
    pad = jnp.zeros((d, LANES - 2 * ne), BF16)
    wr1 = jnp.concatenate([whi, wlo, pad], axis=1)
    wr2 = jnp.concatenate([whi, jnp.zeros((d, ne), BF16), pad], axis=1)
    tok = lambda n: pl.BlockSpec((1, tm, n), lambda bi, i: (bi, i, 0))
    const = lambda a: pl.BlockSpec(a.shape, lambda bi, i: (0, 0), pipeline_mode=pl.Buffered(1))
    vec = pl.BlockSpec((1, d), lambda bi, i: (0, 0))
    return pl.pallas_call(
        _merge_kernel,
        out_shape=[jax.ShapeDtypeStruct((b, l, d), F32),
                   jax.ShapeDtypeStruct((b, l, d), BF16),
                   jax.ShapeDtypeStruct((b, ne, l), F32)],
        grid=(b, l // tm),
        in_specs=[tok(fu.shape[2]), tok(hm.shape[2]), tok(d),
                  pl.BlockSpec((1, N_MOD, d), lambda bi, i: (bi, 0, 0)),
                  vec, const(w_br), const(w_og), const(wf), const(wm), const(wo), vec,
                  const(wr1), const(wr2)],
        out_specs=[tok(d), tok(d), pl.BlockSpec((1, ne, tm), lambda bi, i: (bi, 0, i))],
        compiler_params=_params("parallel", "parallel"),
        name="merge",
    )(fu, hm, x, mods, g1.reshape(1, d), w_br, w_og, wf, wm, wo, g2.reshape(1, d), wr1, wr2)


def _topk_kernel(a_ref, u_ref, slot_ref, before_ref, *, k):
    a = a_ref[...]
    l = a.shape[1]
    total = lambda m: jnp.sum(m, axis=1, keepdims=True)
    ge = lambda bits: jnp.where(a >= pltpu.bitcast(bits, F32), 1.0, 0.0)

    def body(i, t):
        cand = t | lax.shift_left(jnp.int32(1), 30 - i)
        return jnp.where(total(ge(cand)) >= k, cand, t)

    t = lax.fori_loop(0, 31, body, jnp.zeros((a.shape[0], 1), jnp.int32))
    chosen = ge(t + 1)
    rem = ge(t) - chosen
    need = k - total(chosen)
    idx = lax.broadcasted_iota(jnp.int32, a.shape, 1).astype(F32)
    for _ in range(N_TIE_EXTRACT):
        cur = jnp.where(rem > 0.0, a, -1.0)
        is_mx = jnp.where(cur == jnp.max(cur, axis=1, keepdims=True), rem, 0.0)
        first = jnp.min(jnp.where(is_mx > 0.0, idx, float(l)), axis=1, keepdims=True)
        pick = jnp.where(idx == first, jnp.where(need > 0.0, 1.0, 0.0), 0.0)
        chosen = chosen + pick
        rem = rem - pick
        need = need - total(pick)
    before = jnp.dot(rem.astype(BF16), u_ref[...], preferred_element_type=F32)
    sel = chosen + jnp.where(before < need, rem, 0.0)
    rank = jnp.dot(sel.astype(BF16), u_ref[...], preferred_element_type=F32)
    slot_ref[...] = jnp.where(sel > 0.0, rank, -1.0)
    before_ref[...] = rank


def _topk(aff_t, k):
    r, l = aff_t.shape
    tr = min(128, r)
    idx = jnp.arange(l, dtype=jnp.int32)
    upper = (idx[:, None] < idx[None, :]).astype(BF16)
    return pl.pallas_call(
        functools.partial(_topk_kernel, k=k),
        out_shape=[jax.ShapeDtypeStruct((r, l), F32)] * 2,
        grid=(r // tr,),
        in_specs=[pl.BlockSpec((tr, l), lambda i: (i, 0)),
                  pl.BlockSpec((l, l), lambda i: (0, 0), pipeline_mode=pl.Buffered(1))],
        out_specs=[pl.BlockSpec((tr, l), lambda i: (i, 0))] * 2,
        compiler_params=_params("parallel"),
        name="topk",
    )(aff_t, upper)


def _moe_kernel(cnt_ref, hx_ref, srow_ref, arow_ref, slot_ref, wg_ref, wu_ref, wd_ref, x1_ref,
                mod_ref, g_ref, xp_ref, o_ref, y_scr, xs_scr, *, cap, ne, nt):
    bi = pl.program_id(0)
    s = pl.program_id(1)
    ng = hx_ref.shape[0]
    win = MOE_WINDOW
    per_sample = ne * (nt + 1)

    @pl.when(s == 0)
    def _():
        for gi in range(ng):
            y_scr[gi, ne * cap:ne * cap + win, :] = jnp.zeros((win, y_scr.shape[2]), BF16)

    @pl.when(s < ne)
    def _():
        l = hx_ref.shape[1]
        tb, gw = MOE_TILE, MOE_GATHER_WINDOW
        xs_scr[...] = jnp.zeros(xs_scr.shape, F32)
        row_i = lax.broadcasted_iota(jnp.int32, (gw, tb), 0)
        lows, n_pass = [], 0
        for gi in range(ng):
            base = (bi * ng + gi) * per_sample + s * (nt + 1)
            for j in range(nt):
                low = (cnt_ref[base + j] // ROW_ALIGN) * ROW_ALIGN
                lows.append(low)
                n_pass = jnp.maximum(n_pass, (cnt_ref[base + j + 1] - low + gw - 1) // gw)

        def gather_pass(w, carry):
            for gi in range(ng):
                for j in range(nt):
                    lo = pl.multiple_of(jnp.minimum(lows[gi * nt + j] + w * gw, cap), ROW_ALIGN)
                    srow_blk = srow_ref[gi, 0, :, j * tb:(j + 1) * tb]
                    p = jnp.where(srow_blk == (row_i + lo).astype(F32), 1.0, 0.0).astype(BF16)
                    xs_scr[gi, pl.ds(lo, gw), :] += jnp.dot(
                        p, hx_ref[gi, j * tb:(j + 1) * tb, :], preferred_element_type=F32)
            return carry

        gather_pass(0, 0)
        lax.fori_loop(1, n_pass, gather_pass, 0)
        slot_i = lax.broadcasted_iota(jnp.int32, (cap, l), 0).astype(F32)
        for gi in range(ng):
            xs = xs_scr[gi, 0:cap, :].astype(BF16)
            gate = jnp.dot(xs, wg_ref[0], preferred_element_type=F32)
            up = jnp.dot(xs, wu_ref[0], preferred_element_type=F32)
            hid = (_silu(gate) * up).astype(BF16)
            y = jnp.dot(hid, wd_ref[0], preferred_element_type=F32)
            hit = srow_ref[gi, 0] == slot_i
            vals = jnp.sum(jnp.where(hit, arow_ref[gi, 0], 0.0), axis=1, keepdims=True)
            y_scr[gi, pl.ds(pl.multiple_of(s * cap, cap), cap), :] = (y * vals).astype(BF16)

    @pl.when(s >= ne)
    def _():
        tt, d = MOE_TILE, x1_ref.shape[2]
        tps = x1_ref.shape[1] // tt
        gi = (s - ne) // (nt // tps)
        j0 = ((s - ne) % (nt // tps)) * tps
        lane = lax.broadcasted_iota(jnp.int32, (1, ne * win), 1)
        grp, off = lane // win, lane % win
        slots = slot_ref[0].T.astype(BF16)
        tiles, n_pass = [], 0
        for u in range(tps):
            base = (bi * ng + gi) * per_sample + j0 + u
            lows = []
            for e in range(ne):
                start = cnt_ref[base + e * (nt + 1)]
                end = cnt_ref[base + e * (nt + 1) + 1]
                low = (start // SLOT_ALIGN) * SLOT_ALIGN
                lows.append(low)
                n_pass = jnp.maximum(n_pass, (end - low + win - 1) // win)
            sexp = jnp.dot(slots[u * tt:(u + 1) * tt, :], xp_ref[...], preferred_element_type=F32)
            tiles.append((lows, sexp))

        def one_pass(w, accs):
            out = []
            for (lows, sexp), acc in zip(tiles, accs):
                tgt, rows = off, []
                for e in range(ne):
                    lo = jnp.minimum(lows[e] + w * win, cap)
                    tgt = jnp.where(grp == e, off + lo, tgt)
                    rows.append(
                        y_scr[gi, pl.ds(pl.multiple_of(e * cap + lo, SLOT_ALIGN), win), :])
                pt = jnp.where(sexp == tgt.astype(F32), 1.0, 0.0).astype(BF16)
                out.append(acc + jnp.dot(pt, jnp.concatenate(rows, axis=0),
                                         preferred_element_type=F32))
            return tuple(out)

        accs = one_pass(0, tuple(jnp.zeros((tt, d), F32) for _ in range(tps)))
        accs = lax.fori_loop(1, n_pass, one_pass, accs)
        for u, moe in enumerate(accs):
            v = x1_ref[0, u * tt:(u + 1) * tt, :] + mod_ref[gi, 5:6, :] * moe
            ms = jnp.mean(v * v, axis=-1, keepdims=True)
            o_ref[0, u * tt:(u + 1) * tt, :] = v * lax.rsqrt(ms + EPS) * g_ref[...]


def _moe(hx, slot_t, before_t, aff_t, wg, wu, wd, x1, mods, g_final, cap):
    b, l, d = hx.shape
    ne, _, f = wg.shape
    tt = MOE_TILE
    nt = l // tt
    win = MOE_WINDOW
    ng = MOE_GROUP if b % MOE_GROUP == 0 else 1
    assert cap % SLOT_ALIGN == 0 and win % SLOT_ALIGN == 0 and (win & (win - 1)) == 0
    slot_row = slot_t.reshape(b, ne, 1, l)
    aff_row = aff_t.reshape(b, ne, 1, l)
    cnt = jnp.concatenate([before_t[:, :, ::tt], jnp.full((b, ne, 1), cap, F32)], axis=2)
    cnt = cnt.astype(jnp.int32).reshape(b * ne * (nt + 1))
    lane = jnp.arange(ne * win, dtype=jnp.int32)
    expand = (lane[None, :] // win == jnp.arange(ne, dtype=jnp.int32)[:, None]).astype(BF16)
    ex = lambda s: jnp.minimum(s, ne - 1)
    tps = MOE_TILES_PER_STEP if nt % MOE_TILES_PER_STEP == 0 else 1
    ts, spt = tps * tt, nt // tps
    sample = lambda bi, s: bi * ng + jnp.maximum(s - ne, 0) // spt
    tile = lambda s: jnp.maximum(s - ne, 0) % spt
    row = pl.BlockSpec((ng, 1, 1, l), lambda bi, s, c: (bi, ex(s), 0, 0))
    wspec = lambda k, n: pl.BlockSpec((1, k, n), lambda bi, s, c: (ex(s), 0, 0))
    grid_spec = pltpu.PrefetchScalarGridSpec(
        num_scalar_prefetch=1,
        grid=(b // ng, ne + ng * spt),
        in_specs=[pl.BlockSpec((ng, l, d), lambda bi, s, c: (bi, 0, 0),
                               pipeline_mode=pl.Buffered(1)),
                  row, row,
                  pl.BlockSpec((1, ne, ts), lambda bi, s, c: (sample(bi, s), 0, tile(s))),
                  wspec(d, f), wspec(d, f), wspec(f, d),
                  pl.BlockSpec((1, ts, d), lambda bi, s, c: (sample(bi, s), tile(s), 0)),
                  pl.BlockSpec((ng, N_MOD, d), lambda bi, s, c: (bi, 0, 0)),
                  pl.BlockSpec((1, d), lambda bi, s, c: (0, 0)),
                  pl.BlockSpec((ne, ne * win), lambda bi, s, c: (0, 0))],
        out_specs=pl.BlockSpec((1, ts, d), lambda bi, s, c: (sample(bi, s), tile(s), 0)),
        scratch_shapes=[pltpu.VMEM((ng, ne * cap + win, d), BF16),
                        pltpu.VMEM((ng, cap + MOE_GATHER_WINDOW, d), F32)])
    return pl.pallas_call(
        functools.partial(_moe_kernel, cap=cap, ne=ne, nt=nt),
        out_shape=jax.ShapeDtypeStruct((b, l, d), F32),
        grid_spec=grid_spec,
        compiler_params=_params("parallel", "arbitrary"),
        name="moe",
    )(cnt, hx, slot_row, aff_row, slot_t, wg, wu, wd, x1, mods, g_final.reshape(1, d), expand)


def kernel(x, c, ctx, c_ctx, w_ada, b_ada, g_norm1, w_in, b_gates, w_conv, w_fourier, w_mlstm,
           w_out, g_norm2, w_router, w_gate_e, w_up_e, w_down_e, g_final):
    assert w_ada.shape[0] == 1, "single-layer stack"
    b, l, d = x.shape
    lc = ctx.shape[1]
    d_f = d // 2
    d_m = d // 2
    dh = d_m // N_H
    ne = w_router.shape[2]
    cap = CAP_FACTOR * l // ne
    off_qk = d_f
    off_v = off_qk + 2 * d_m
    off_o = off_v + d_m
    off_g = off_o + d_m
    off_br = off_g + N_GATE_KINDS * N_H

    n_c = b + 1
    n_pad = -n_c % 8
    c_all = jnp.concatenate([c, c_ctx[None, :], jnp.zeros((n_pad, d), F32)], axis=0)
    mods = _ada(c_all, w_ada[0], b_ada[0]).reshape(n_c + n_pad, N_MOD, d)

    w = w_in[0]
    wb = w.astype(BF16)
    w_f, w_qk, w_v, w_o, w_br = (wb[:, 0:off_qk], wb[:, off_qk:off_v], wb[:, off_v:off_o],
                                 wb[:, off_o:off_g], wb[:, off_br:])
    w_gt = wb[:, off_g:off_br].T
    w9 = w_conv[0].reshape(9, 2 * d_m)
    qk_scale = jnp.concatenate([jnp.ones((1, d_m), F32), jnp.full((1, d_m), dh ** -0.5, F32)], axis=1)

    pc_qk, pc_v, gc_t = _inproj(ctx, mods, lambda bi, i: (b, 0, 0), g_norm1[0],
                                [w_qk, w_v], w_gt, tm=lc)
    q_c = _conv(pc_qk, w9, qk_scale, 1, 0, d_m, False)
    kt_c = _conv(pc_qk, w9, qk_scale, 1, d_m, 2 * d_m, True)
    rr_c = _gateprep(gc_t, b_gates[0], min(SCAN_CHUNK, lc))
    c0 = jnp.zeros((b, 2, N_H, dh, 2 * dh), F32)
    m0 = jnp.zeros((b, 2, N_H, 1, LANES), F32)
    c_seed, m_seed = _scan(q_c, kt_c, pc_v, rr_c, c0, m0, hps=N_H, want_h=False,
                           want_state=True)

    p_f, p_qk, p_v, g_t = _inproj(x, mods, lambda bi, i: (bi, 0, 0), g_norm1[0],
                                  [w_f, w_qk, w_v], w_gt, tm=1024)
    q_x = _conv(p_qk, w9, qk_scale, l // GRID_W, 0, d_m, False)
    kt_x = _conv(p_qk, w9, qk_scale, l // GRID_W, d_m, 2 * d_m, True)
    rr_x = _gateprep(g_t, b_gates[0], SCAN_CHUNK)
    (hm,) = _scan(q_x, kt_x, p_v, rr_x, c_seed, m_seed, hps=SCAN_HEADS_PER_STEP, want_h=True,
                  want_state=False)
    fu = _fourier(p_f)
    x1, hx2, aff_t = _merge(fu, hm, x, mods, g_norm1[0], w_br, w_o, w_fourier[0].astype(BF16),
                            w_mlstm[0].astype(BF16), w_out[0].astype(BF16), g_norm2[0], w_router[0],
                            tm=1024)

    slot_t, before_t = _topk(aff_t.reshape(b * ne, l), cap)
    return _moe(hx2, slot_t.reshape(b, ne, l), before_t.reshape(b, ne, l), aff_t,
                w_gate_e[0].astype(BF16), w_up_e[0].astype(BF16), w_down_e[0].astype(BF16),
                x1, mods, g_final, cap)
```

```python
import functools
import math

import jax
import jax.numpy as jnp
from jax import lax
from jax.experimental import pallas as pl
from jax.experimental.pallas import tpu as pltpu

F32 = jnp.float32
BF16 = jnp.bfloat16

GRID_W = 64
N_FG = 4
N_H = 4
N_GATE_KINDS = 4
N_EXPERTS = 16
CAP_FACTOR = 2
EPS = 1e-6
LOG2E = math.log2(math.e)
N_MOD = 6

LANES = 128
V7X_VMEM_BYTES = 64 * 1024 * 1024
VMEM_LIMIT = V7X_VMEM_BYTES - 8 * 1024 * 1024

SCAN_CHUNK = 256
SCAN_HEADS_PER_STEP = 1
N_SCAN_ROWS = 8
N_TIE_EXTRACT = 3
MERGE_SUBTILE = 256
MOE_GROUP = 2
MOE_TILE = 256
MOE_TILES_PER_STEP = 2
MOE_WINDOW = 64
MOE_GATHER_WINDOW = 128
ROW_ALIGN = 8
SLOT_ALIGN = 16


def _params(*sem):
    return pltpu.CompilerParams(dimension_semantics=sem, vmem_limit_bytes=VMEM_LIMIT)


def _silu(v):
    return v * jax.nn.sigmoid(v)


def _rms_mod(v, g, shift, scale):
    ms = jnp.mean(v * v, axis=-1, keepdims=True)
    return (v * lax.rsqrt(ms + EPS) * g) * (1.0 + scale) + shift


def _ada_kernel(c_ref, w_ref, b_ref, o_ref):
    s = _silu(c_ref[...])
    o_ref[...] = jnp.dot(s, w_ref[...], preferred_element_type=F32,
                         precision=lax.Precision.HIGHEST) + b_ref[...]


def _ada(c_all, w_ada, b_ada):
    n, d = c_all.shape
    cols = w_ada.shape[1]
    tn = 512
    return pl.pallas_call(
        _ada_kernel,
        out_shape=jax.ShapeDtypeStruct((n, cols), F32),
        grid=(cols // tn,),
        in_specs=[pl.BlockSpec((n, d), lambda j: (0, 0)),
                  pl.BlockSpec((d, tn), lambda j: (0, j)),
                  pl.BlockSpec((1, tn), lambda j: (0, j))],
        out_specs=pl.BlockSpec((n, tn), lambda j: (0, j)),
        compiler_params=_params("parallel"),
        name="ada",
    )(c_all, w_ada, b_ada.reshape(1, cols))


def _inproj_kernel(x_ref, mod_ref, g_ref, *refs, n_seg):
    w_refs = refs[:n_seg]
    wgt_ref = refs[n_seg]
    o_refs = refs[n_seg + 1:2 * n_seg + 1]
    gt_ref = refs[2 * n_seg + 1]
    h = _rms_mod(x_ref[0], g_ref[...], mod_ref[0, 0:1, :], mod_ref[0, 1:2, :])
    hb = h.astype(BF16)
    for w_ref, o_ref in zip(w_refs, o_refs):
        o_ref[0] = jnp.dot(hb, w_ref[...], preferred_element_type=F32).astype(o_ref.dtype)
    gt_ref[0] = lax.dot_general(wgt_ref[...], hb, (((1,), (1,)), ((), ())),
                                preferred_element_type=F32)


def _inproj(x, mods, mod_row, g, w_segs, w_gt, tm):
    b, l, d = x.shape
    tm = min(tm, l)
    n_seg = len(w_segs)
    ng = w_gt.shape[0]
    const2 = lambda bi, i: (0, 0)
    in_specs = [pl.BlockSpec((1, tm, d), lambda bi, i: (bi, i, 0)),
                pl.BlockSpec((1, N_MOD, d), mod_row),
                pl.BlockSpec((1, d), const2)]
    in_specs += [pl.BlockSpec(w.shape, const2, pipeline_mode=pl.Buffered(1)) for w in w_segs]
    in_specs += [pl.BlockSpec(w_gt.shape, const2)]
    out_shape = [jax.ShapeDtypeStruct((b, l, w.shape[1]), BF16) for w in w_segs]
    out_shape += [jax.ShapeDtypeStruct((b, ng, l), F32)]
    out_specs = [pl.BlockSpec((1, tm, w.shape[1]), lambda bi, i: (bi, i, 0)) for w in w_segs]
    out_specs += [pl.BlockSpec((1, ng, tm), lambda bi, i: (bi, 0, i))]
    return pl.pallas_call(
        functools.partial(_inproj_kernel, n_seg=n_seg),
        out_shape=out_shape,
        grid=(b, l // tm),
        in_specs=in_specs,
        out_specs=out_specs,
        compiler_params=_params("parallel", "parallel"),
        name="inproj",
    )(x, mods, g.reshape(1, d), *w_segs, w_gt)


def _conv_kernel(p_ref, w_ref, s_ref, o_ref, *, rows, width, transpose):
    v = p_ref[0].astype(F32)
    l, ct = v.shape
    grid = lambda a: a.reshape(rows, width, ct)
    v3, vl3, vr3 = grid(v), grid(pltpu.roll(v, 1, 0)), grid(pltpu.roll(v, l - 1, 0))
    col = lax.broadcasted_iota(jnp.int32, (1, width, ct), 1)
    w = w_ref[...]
    wl = [jnp.where(col == 0, 0.0, w[3 * dr:3 * dr + 1, :][None]) for dr in range(3)]
    wr = [jnp.where(col == width - 1, 0.0, w[3 * dr + 2:3 * dr + 3, :][None]) for dr in range(3)]

    def tap(dr):
        return vl3 * wl[dr] + v3 * w[3 * dr + 1:3 * dr + 2, :][None] + vr3 * wr[dr]

    acc = tap(1)
    if rows > 1:
        edge = jnp.zeros((1, width, ct), F32)
        acc = acc + jnp.concatenate([edge, tap(0)[:-1]], axis=0)
        acc = acc + jnp.concatenate([tap(2)[1:], edge], axis=0)
    acc = acc.reshape(l, ct)
    y = _silu(acc) * s_ref[...]
    o_ref[0] = (y.T if transpose else y).astype(o_ref.dtype)


def _conv(p, w9, scale, rows, col_lo, col_hi, transpose):
    b, l, _ = p.shape
    ct = 512
    j0 = col_lo // ct
    n = col_hi - col_lo
    if transpose:
        out_shape = jax.ShapeDtypeStruct((b, n, l), BF16)
        out_spec = pl.BlockSpec((1, ct, l), lambda bi, j: (bi, j, 0))
    else:
        out_shape = jax.ShapeDtypeStruct((b, l, n), BF16)
        out_spec = pl.BlockSpec((1, l, ct), lambda bi, j: (bi, 0, j))
    return pl.pallas_call(
        functools.partial(_conv_kernel, rows=rows, width=l // rows, transpose=transpose),
        out_shape=out_shape,
        grid=(b, n // ct),
        in_specs=[pl.BlockSpec((1, l, ct), lambda bi, j: (bi, 0, j + j0)),
                  pl.BlockSpec((9, ct), lambda bi, j: (0, j + j0)),
                  pl.BlockSpec((1, ct), lambda bi, j: (0, j + j0))],
        out_specs=out_spec,
        compiler_params=_params("parallel", "parallel"),
        name="conv_kt" if transpose else "conv_q",
    )(p, w9, scale)


def _log_sigmoid(v):
    return jnp.minimum(v, 0.0) - jnp.log1p(jnp.exp(-jnp.abs(v)))


def _chunk_scan(v, op, ident, chunk, reverse):
    l = v.shape[1]
    pos = lax.broadcasted_iota(jnp.int32, v.shape, 1) % chunk
    s = 1
    while s < chunk:
        if reverse:
            shifted = jnp.where(pos < chunk - s, pltpu.roll(v, l - s, 1), ident)
        else:
            shifted = jnp.where(pos >= s, pltpu.roll(v, s, 1), ident)
        v = op(v, shifted)
        s *= 2
    return v


def _gateprep_kernel(g_ref, b_ref, o_ref, *, chunk):
    g = g_ref[0] + b_ref[...]
    i_f = g[0:N_H] * LOG2E
    f_f = _log_sigmoid(g[N_H:2 * N_H]) * LOG2E
    i_b = g[2 * N_H:3 * N_H] * LOG2E
    f_b = _log_sigmoid(g[3 * N_H:4 * N_H]) * LOG2E
    neg = -jnp.inf
    cum_f = _chunk_scan(f_f, jnp.add, 0.0, chunk, False)
    br_f = i_f - cum_f
    pm_f = _chunk_scan(br_f, jnp.maximum, neg, chunk, False)
    cum_b = _chunk_scan(f_b, jnp.add, 0.0, chunk, True)
    br_b = i_b - cum_b
    pm_b = _chunk_scan(br_b, jnp.maximum, neg, chunk, True)
    zero = jnp.zeros_like(cum_f[0:1])
    for h in range(N_H):
        o_ref[0, h] = jnp.concatenate(
            [a[h:h + 1] for a in (cum_f, br_f, pm_f, cum_b, br_b, pm_b)] + [zero, zero], axis=0)


def _gateprep(g_t, b_gates, chunk):
    b, ng, l = g_t.shape
    return pl.pallas_call(
        functools.partial(_gateprep_kernel, chunk=chunk),
        out_shape=jax.ShapeDtypeStruct((b, N_H, N_SCAN_ROWS, l), F32),
        grid=(b,),
        in_specs=[pl.BlockSpec((1, ng, l), lambda bi: (bi, 0, 0)),
                  pl.BlockSpec((ng, 1), lambda bi: (0, 0))],
        out_specs=pl.BlockSpec((1, N_H, N_SCAN_ROWS, l), lambda bi: (bi, 0, 0, 0)),
        compiler_params=_params("parallel"),
        name="gateprep",
    )(g_t, b_gates.reshape(ng, 1))


def _scan_kernel(*refs, chunk, hps, want_h, want_state):
    q_ref, kt_ref, v_ref, rr_ref, c0_ref, m0_ref = refs[:6]
    pos = 6
    if want_h:
        h_ref = refs[pos]
        pos += 1
    if want_state:
        cs_ref, ms_ref = refs[pos:pos + 2]
        pos += 2
    vext_scr = refs[pos]
    if want_h:
        hf_scr, hb_scr = refs[pos + 1:pos + 3]

    l, dh = q_ref.shape[1], q_ref.shape[2] // hps
    nc = l // chunk
    rep = chunk // dh
    jj = lax.broadcasted_iota(jnp.int32, (chunk, chunk), 0)
    ss = lax.broadcasted_iota(jnp.int32, (chunk, chunk), 1)
    causal = (jj >= ss, jj <= ss)

    for hh in range(hps):
        hl = slice(hh * dh, (hh + 1) * dh)
        vext_scr[hh, :, 0:dh] = v_ref[0, :, hl]
        vext_scr[hh, :, dh:2 * dh] = jnp.ones((l, dh), BF16)
        cols = rr_ref[0, hh].T
        cst = [c0_ref[0, d, hh] for d in range(2)]
        mst = [m0_ref[0, d, hh][:, 0:1] for d in range(2)]

        for i in range(nc):
            for d in range(2):
                c = i if d == 0 else nc - 1 - i
                lo = c * chunk
                end = chunk - 1 if d == 0 else 0
                r0 = 3 * d
                qc = q_ref[0, lo:lo + chunk, hl]
                ktc = kt_ref[0, hl, lo:lo + chunk]
                vc = vext_scr[hh, lo:lo + chunk, :]
                cum_c = cols[lo:lo + chunk, r0:r0 + 1]
                pm_c = cols[lo:lo + chunk, r0 + 2:r0 + 3]
                br_r = rr_ref[0, hh, r0 + 1:r0 + 2, lo:lo + chunk]
                tot = rr_ref[0, hh, r0:r0 + 1, lo + end:lo + end + 1]
                pm_end = rr_ref[0, hh, r0 + 2:r0 + 3, lo + end:lo + end + 1]
                m0 = mst[d]
                mm = jnp.broadcast_to(jnp.maximum(m0, pm_c), (chunk, dh))
                d_in = jnp.where(causal[d],
                                 jnp.exp2(br_r - jnp.concatenate([mm] * rep, axis=1)), 0.0)
                d_st = jnp.exp2(m0 - mm)
                s = jnp.dot(qc, ktc, preferred_element_type=F32)
                p = (s * d_in).astype(BF16)
                qd = (qc.astype(F32) * d_st).astype(BF16)
                both = (jnp.dot(p, vc, preferred_element_type=F32)
                        + jnp.dot(qd, cst[d].astype(BF16), preferred_element_type=F32))
                if want_h:
                    m_j = jnp.broadcast_to(cum_c, (chunk, dh)) + mm
                    hc = both[:, 0:dh] / jnp.maximum(jnp.abs(both[:, dh:2 * dh]), jnp.exp2(-m_j))
                    (hf_scr if d == 0 else hb_scr)[hh, lo:lo + chunk, :] = hc
                kw = (ktc.astype(F32) * d_in[end:end + 1, :]).astype(BF16)
                upd = jnp.dot(kw, vc, preferred_element_type=F32)
                dn = d_st[end:end + 1, :]
                cst[d] = jnp.concatenate([dn, dn], axis=1) * cst[d] + upd
                mst[d] = tot + jnp.maximum(m0, pm_end)

        if want_h:
            h_ref[0, :, hl] = (hf_scr[hh] + hb_scr[hh]).astype(h_ref.dtype)
        if want_state:
            for d in range(2):
                cs_ref[0, d, hh] = cst[d]
                ms_ref[0, d, hh] = jnp.broadcast_to(mst[d], (1, LANES))


def _scan(q, kt, v, rr, c0, m0, *, hps, want_h, want_state):
    b, l, dm = v.shape
    dh = dm // N_H
    chunk = min(SCAN_CHUNK, l)
    assert chunk % dh == 0 and l % chunk == 0 and N_H % hps == 0
    tok = pl.BlockSpec((1, l, hps * dh), lambda bi, h: (bi, 0, h))
    in_specs = [tok, pl.BlockSpec((1, hps * dh, l), lambda bi, h: (bi, h, 0)), tok]
    args = [q, kt, v]
    st_c = pl.BlockSpec((1, 2, hps, dh, 2 * dh), lambda bi, h: (bi, 0, h, 0, 0))
    st_m = pl.BlockSpec((1, 2, hps, 1, LANES), lambda bi, h: (bi, 0, h, 0, 0))
    in_specs += [pl.BlockSpec((1, hps, N_SCAN_ROWS, l), lambda bi, h: (bi, h, 0, 0)), st_c, st_m]
    args += [rr, c0, m0]
    out_shape, out_specs = [], []
    scratch = [pltpu.VMEM((hps, l, 2 * dh), BF16)]
    if want_h:
        out_shape.append(jax.ShapeDtypeStruct((b, l, dm), BF16))
        out_specs.append(tok)
        scratch += [pltpu.VMEM((hps, l, dh), F32), pltpu.VMEM((hps, l, dh), F32)]
    if want_state:
        out_shape += [jax.ShapeDtypeStruct(c0.shape, F32), jax.ShapeDtypeStruct(m0.shape, F32)]
        out_specs += [st_c, st_m]
    return pl.pallas_call(
        functools.partial(_scan_kernel, chunk=chunk, hps=hps, want_h=want_h, want_state=want_state),
        out_shape=out_shape,
        grid=(b, N_H // hps),
        in_specs=in_specs,
        out_specs=out_specs,
        scratch_shapes=scratch,
        compiler_params=_params("parallel", "parallel"),
        name="scan_h" if want_h else "scan_state",
    )(*args)


def _twiddle(z, k, n):
    zr, zi = z
    k %= n
    if k == 0:
        return z
    if 4 * k == n:
        return zi, -zr
    if 2 * k == n:
        return -zr, -zi
    if 4 * k == 3 * n:
        return -zi, zr
    wr, wi = math.cos(2.0 * math.pi * k / n), -math.sin(2.0 * math.pi * k / n)
    return zr * wr - zi * wi, zr * wi + zi * wr


def _fft_list(xs):
    n = len(xs)
    if n == 1:
        return xs
    ev, od = _fft_list(xs[0::2]), _fft_list(xs[1::2])
    out = [None] * n
    for k in range(n // 2):
        tr, ti = _twiddle(od[k], k, n)
        out[k] = (ev[k][0] + tr, ev[k][1] + ti)
        out[k + n // 2] = (ev[k][0] - tr, ev[k][1] - ti)
    return out


def _fourier_kernel(u_ref, wc_ref, tw_ref, cs_ref, o_ref, zz_scr, r_scr, *, fg, n1):
    l, df = u_ref.shape[1], u_ref.shape[2]
    n2 = l // n1
    for g in range(df // fg):
        r = jnp.dot(u_ref[0, :, g * fg:(g + 1) * fg], wc_ref[...], preferred_element_type=F32)
        xs = [(r[j * n2:(j + 1) * n2, 0:fg], r[j * n2:(j + 1) * n2, fg:2 * fg]) for j in range(n1)]
        zs = _fft_list(xs)
        for k1 in range(n1):
            zr, zi = zs[k1]
            if k1 > 0:
                tc, ts = tw_ref[0, k1], tw_ref[1, k1]
                zr, zi = zr * tc + zi * ts, zi * tc - zr * ts
            zz_scr[k1, 0:n2, g * fg:(g + 1) * fg] = zr.astype(BF16)
            zz_scr[k1, n2:2 * n2, g * fg:(g + 1) * fg] = zi.astype(BF16)
    pitch = r_scr.shape[1] // n1
    for k1 in range(n1):
        res = jnp.dot(cs_ref[...], zz_scr[k1], preferred_element_type=F32)
        for c in range(df // LANES):
            r_scr[c, k1 * pitch:k1 * pitch + n2, :] = res[:, c * LANES:(c + 1) * LANES]

    def emit(j, carry):
        for c in range(df // LANES):
            a = r_scr[c, pl.ds(2 * j, n1, stride=pitch), :]
            b = r_scr[c, pl.ds(2 * j + 1, n1, stride=pitch), :]
            o_ref[0, pl.ds(pl.multiple_of(2 * n1 * j, 2 * n1), 2 * n1), c * LANES:(c + 1) * LANES] = (
                jnp.concatenate([a, b], axis=0).astype(o_ref.dtype))
        return carry

    lax.fori_loop(0, n2 // 2, emit, 0, unroll=8)


def _dft_angles(rows, cols, n):
    ang = ((rows[:, None] * cols[None, :]) % n).astype(F32) * (2.0 * math.pi / n)
    return jnp.cos(ang), jnp.sin(ang)


def _fourier(u):
    b, l, df = u.shape
    fg = df // N_FG
    n1 = 8
    n2 = l // n1
    ar = lambda n: jnp.arange(n, dtype=jnp.int32)
    cc, sc = _dft_angles(ar(fg), ar(fg), fg)
    wc = (jnp.concatenate([cc, -sc], axis=1) * fg ** -0.5).astype(BF16)
    tc, ts = _dft_angles(ar(n1), ar(n2), l)
    tw = jnp.broadcast_to(jnp.stack([tc, ts])[..., None], (2, n1, n2, fg))
    c2, s2 = _dft_angles(ar(n2), ar(n2), n2)
    cs = (jnp.concatenate([c2, s2], axis=1) * l ** -0.5).astype(BF16)
    const = lambda a: pl.BlockSpec(a.shape, lambda bi: (0,) * a.ndim, pipeline_mode=pl.Buffered(1))
    pitch = n2 + 4
    return pl.pallas_call(
        functools.partial(_fourier_kernel, fg=fg, n1=n1),
        out_shape=jax.ShapeDtypeStruct((b, l, df), BF16),
        grid=(b,),
        in_specs=[pl.BlockSpec((1, l, df), lambda bi: (bi, 0, 0)), const(wc), const(tw), const(cs)],
        out_specs=pl.BlockSpec((1, l, df), lambda bi: (bi, 0, 0)),
        scratch_shapes=[pltpu.VMEM((n1, 2 * n2, df), BF16),
                        pltpu.VMEM((df // LANES, n1 * pitch, LANES), F32)],
        compiler_params=_params("parallel"),
        name="fourier",
    )(u, wc, tw, cs)


def _merge_kernel(fu_ref, hm_ref, x_ref, mod_ref, g1_ref, wbr_ref, wog_ref, wf_ref, wm_ref, wo_ref,
                  g2_ref, wr1_ref, wr2_ref, x1_ref, hx_ref, aff_ref):
    tm, d = x_ref.shape[1], x_ref.shape[2]
    ne = aff_ref.shape[1]
    for r0 in range(0, tm, MERGE_SUBTILE):
        rows = slice(r0, r0 + MERGE_SUBTILE)
        x = x_ref[0, rows, :]
        h1 = _rms_mod(x, g1_ref[...], mod_ref[0, 0:1, :], mod_ref[0, 1:2, :]).astype(BF16)
        br = jnp.dot(h1, wbr_ref[...], preferred_element_type=F32)
        og = jnp.dot(h1, wog_ref[...], preferred_element_type=F32)
        yf = jnp.dot(fu_ref[0, rows, :], wf_ref[...], preferred_element_type=F32)
        hm = (jax.nn.sigmoid(og) * hm_ref[0, rows, :].astype(F32)).astype(BF16)
        ym = jnp.dot(hm, wm_ref[...], preferred_element_type=F32)
        merged = (jax.nn.sigmoid(br[:, 0:d]) * yf + jax.nn.sigmoid(br[:, d:2 * d]) * ym).astype(BF16)
        o = jnp.dot(merged, wo_ref[...], preferred_element_type=F32)
        x1 = x + mod_ref[0, 2:3, :] * o
        x1_ref[0, rows, :] = x1
        h2 = _rms_mod(x1, g2_ref[...], mod_ref[0, 3:4, :], mod_ref[0, 4:5, :])
        hi = h2.astype(BF16)
        hx_ref[0, rows, :] = hi
        lo = (h2 - hi.astype(F32)).astype(BF16)
        r = (jnp.dot(hi, wr1_ref[...], preferred_element_type=F32)
             + jnp.dot(lo, wr2_ref[...], preferred_element_type=F32))
        rt = r.T
        logits = rt[0:ne, :] + rt[ne:2 * ne, :]
        e = jnp.exp(logits - jnp.max(logits, axis=0, keepdims=True))
        aff_ref[0, :, rows] = e / jnp.sum(e, axis=0, keepdims=True)


def _merge(fu, hm, x, mods, g1, w_br, w_og, wf, wm, wo, g2, w_router, tm):
    b, l, d = x.shape
    tm = min(tm, l)
    ne = w_router.shape[1]
    whi = w_router.astype(BF16)
    wlo = (w_router - whi.astype(F32)).astype(BF16)
    pad = jnp.zeros((d, LANES - 2 * ne), BF16)
    wr1 = jnp.concatenate([whi, wlo, pad], axis=1)
    wr2 = jnp.concatenate([whi, jnp.zeros((d, ne), BF16), pad], axis=1)
    tok = lambda n: pl.BlockSpec((1, tm, n), lambda bi, i: (bi, i, 0))
    const = lambda a: pl.BlockSpec(a.shape, lambda bi, i: (0, 0), pipeline_mode=pl.Buffered(1))
    vec = pl.BlockSpec((1, d), lambda bi, i: (0, 0))
    return pl.pallas_call(
        _merge_kernel,
        out_shape=[jax.ShapeDtypeStruct((b, l, d), F32),
                   jax.ShapeDtypeStruct((b, l, d), BF16),
                   jax.ShapeDtypeStruct((b, ne, l), F32)],
        grid=(b, l // tm),
        in_specs=[tok(fu.shape[2]), tok(hm.shape[2]), tok(d),
                  pl.BlockSpec((1, N_MOD, d), lambda bi, i: (bi, 0, 0)),
                  vec, const(w_br), const(w_og), const(wf), const(wm), const(wo), vec,
                  const(wr1), const(wr2)],
        out_specs=[tok(d), tok(d), pl.BlockSpec((1, ne, tm), lambda bi, i: (bi, 0, i))],
        compiler_params=_params("parallel", "parallel"),
        name="merge",
    )(fu, hm, x, mods, g1.reshape(1, d), w_br, w_og, wf, wm, wo, g2.reshape(1, d), wr1, wr2)


def _topk_kernel(a_ref, u_ref, slot_ref, before_ref, *, k):
    a = a_ref[...]
    l = a.shape[1]
    total = lambda m: jnp.sum(m, axis=1, keepdims=True)
    ge = lambda bits: jnp.where(a >= pltpu.bitcast(bits, F32), 1.0, 0.0)

    def body(i, t):
        cand = t | lax.shift_left(jnp.int32(1), 30 - i)
        return jnp.where(total(ge(cand)) >= k, cand, t)

    t = lax.fori_loop(0, 31, body, jnp.zeros((a.shape[0], 1), jnp.int32))
    chosen = ge(t + 1)
    rem = ge(t) - chosen
    need = k - total(chosen)
    idx = lax.broadcasted_iota(jnp.int32, a.shape, 1).astype(F32)
    for _ in range(N_TIE_EXTRACT):
        cur = jnp.where(rem > 0.0, a, -1.0)
        is_mx = jnp.where(cur == jnp.max(cur, axis=1, keepdims=True), rem, 0.0)
        first = jnp.min(jnp.where(is_mx > 0.0, idx, float(l)), axis=1, keepdims=True)
        pick = jnp.where(idx == first, jnp.where(need > 0.0, 1.0, 0.0), 0.0)
        chosen = chosen + pick
        rem = rem - pick
        need = need - total(pick)
    before = jnp.dot(rem.astype(BF16), u_ref[...], preferred_element_type=F32)
    sel = chosen + jnp.where(before < need, rem, 0.0)
    rank = jnp.dot(sel.astype(BF16), u_ref[...], preferred_element_type=F32)
    slot_ref[...] = jnp.where(sel > 0.0, rank, -1.0)
    before_ref[...] = rank


def _topk(aff_t, k):
    r, l = aff_t.shape
    tr = min(128, r)
    idx = jnp.arange(l, dtype=jnp.int32)
    upper = (idx[:, None] < idx[None, :]).astype(BF16)
    return pl.pallas_call(
        functools.partial(_topk_kernel, k=k),
        out_shape=[jax.ShapeDtypeStruct((r, l), F32)] * 2,
        grid=(r // tr,),
        in_specs=[pl.BlockSpec((tr, l), lambda i: (i, 0)),
                  pl.BlockSpec((l, l), lambda i: (0, 0), pipeline_mode=pl.Buffered(1))],
        out_specs=[pl.BlockSpec((tr, l), lambda i: (i, 0))] * 2,
        compiler_params=_params("parallel"),
        name="topk",
    )(aff_t, upper)


def _moe_kernel(cnt_ref, hx_ref, srow_ref, arow_ref, slot_ref, wg_ref, wu_ref, wd_ref, x1_ref,
                mod_ref, g_ref, xp_ref, o_ref, y_scr, xs_scr, *, cap, ne, nt):
    bi = pl.program_id(0)
    s = pl.program_id(1)
    ng = hx_ref.shape[0]
    win = MOE_WINDOW
    per_sample = ne * (nt + 1)

    @pl.when(s == 0)
    def _():
        for gi in range(ng):
            y_scr[gi, ne * cap:ne * cap + win, :] = jnp.zeros((win, y_scr.shape[2]), BF16)

    @pl.when(s < ne)
    def _():
        l = hx_ref.shape[1]
        tb, gw = MOE_TILE, MOE_GATHER_WINDOW
        xs_scr[...] = jnp.zeros(xs_scr.shape, F32)
        row_i = lax.broadcasted_iota(jnp.int32, (gw, tb), 0)
        lows, n_pass = [], 0
        for gi in range(ng):
            base = (bi * ng + gi) * per_sample + s * (nt + 1)
            for j in range(nt):
                low = (cnt_ref[base + j] // ROW_ALIGN) * ROW_ALIGN
                lows.append(low)
                n_pass = jnp.maximum(n_pass, (cnt_ref[base + j + 1] - low + gw - 1) // gw)

        def gather_pass(w, carry):
            for gi in range(ng):
                for j in range(nt):
                    lo = pl.multiple_of(jnp.minimum(lows[gi * nt + j] + w * gw, cap), ROW_ALIGN)
                    srow_blk = srow_ref[gi, 0, :, j * tb:(j + 1) * tb]
                    p = jnp.where(srow_blk == (row_i + lo).astype(F32), 1.0, 0.0).astype(BF16)
                    xs_scr[gi, pl.ds(lo, gw), :] += jnp.dot(
                        p, hx_ref[gi, j * tb:(j + 1) * tb, :], preferred_element_type=F32)
            return carry

        gather_pass(0, 0)
        lax.fori_loop(1, n_pass, gather_pass, 0)
        slot_i = lax.broadcasted_iota(jnp.int32, (cap, l), 0).astype(F32)
        for gi in range(ng):
            xs = xs_scr[gi, 0:cap, :].astype(BF16)
            gate = jnp.dot(xs, wg_ref[0], preferred_element_type=F32)
            up = jnp.dot(xs, wu_ref[0], preferred_element_type=F32)
            hid = (_silu(gate) * up).astype(BF16)
            y = jnp.dot(hid, wd_ref[0], preferred_element_type=F32)
            hit = srow_ref[gi, 0] == slot_i
            vals = jnp.sum(jnp.where(hit, arow_ref[gi, 0], 0.0), axis=1, keepdims=True)
            y_scr[gi, pl.ds(pl.multiple_of(s * cap, cap), cap), :] = (y * vals).astype(BF16)

    @pl.when(s >= ne)
    def _():
        tt, d = MOE_TILE, x1_ref.shape[2]
        tps = x1_ref.shape[1] // tt
        gi = (s - ne) // (nt // tps)
        j0 = ((s - ne) % (nt // tps)) * tps
        lane = lax.broadcasted_iota(jnp.int32, (1, ne * win), 1)
        grp, off = lane // win, lane % win
        slots = slot_ref[0].T.astype(BF16)
        tiles, n_pass = [], 0
        for u in range(tps):
            base = (bi * ng + gi) * per_sample + j0 + u
            lows = []
            for e in range(ne):
                start = cnt_ref[base + e * (nt + 1)]
                end = cnt_ref[base + e * (nt + 1) + 1]
                low = (start // SLOT_ALIGN) * SLOT_ALIGN
                lows.append(low)
                n_pass = jnp.maximum(n_pass, (end - low + win - 1) // win)
            sexp = jnp.dot(slots[u * tt:(u + 1) * tt, :], xp_ref[...], preferred_element_type=F32)
            tiles.append((lows, sexp))

        def one_pass(w, accs):
            out = []
            for (lows, sexp), acc in zip(tiles, accs):
                tgt, rows = off, []
                for e in range(ne):
                    lo = jnp.minimum(lows[e] + w * win, cap)
                    tgt = jnp.where(grp == e, off + lo, tgt)
                    rows.append(
                        y_scr[gi, pl.ds(pl.multiple_of(e * cap + lo, SLOT_ALIGN), win), :])
                pt = jnp.where(sexp == tgt.astype(F32), 1.0, 0.0).astype(BF16)
                out.append(acc + jnp.dot(pt, jnp.concatenate(rows, axis=0),
                                         preferred_element_type=F32))
            return tuple(out)

        def finish(extra_passes):
            accs = one_pass(0, tuple(jnp.zeros((tt, d), F32) for _ in range(tps)))
            if extra_passes:
                accs = lax.fori_loop(1, n_pass, one_pass, accs)
            for u, moe in enumerate(accs):
                v = x1_ref[0, u * tt:(u + 1) * tt, :] + mod_ref[gi, 5:6, :] * moe
                ms = jnp.mean(v * v, axis=-1, keepdims=True)
                o_ref[0, u * tt:(u + 1) * tt, :] = v * lax.rsqrt(ms + EPS) * g_ref[...]

        pl.when(n_pass <= 1)(lambda: finish(False))
        pl.when(n_pass > 1)(lambda: finish(True))


def _moe(hx, slot_t, before_t, aff_t, wg, wu, wd, x1, mods, g_final, cap):
    b, l, d = hx.shape
    ne, _, f = wg.shape
    tt = MOE_TILE
    nt = l // tt
    win = MOE_WINDOW
    ng = MOE_GROUP if b % MOE_GROUP == 0 else 1
    assert cap % SLOT_ALIGN == 0 and win % SLOT_ALIGN == 0 and (win & (win - 1)) == 0
    slot_row = slot_t.reshape(b, ne, 1, l)
    aff_row = aff_t.reshape(b, ne, 1, l)
    cnt = jnp.concatenate([before_t[:, :, ::tt], jnp.full((b, ne, 1), cap, F32)], axis=2)
    cnt = cnt.astype(jnp.int32).reshape(b * ne * (nt + 1))
    lane = jnp.arange(ne * win, dtype=jnp.int32)
    expand = (lane[None, :] // win == jnp.arange(ne, dtype=jnp.int32)[:, None]).astype(BF16)
    ex = lambda s: jnp.minimum(s, ne - 1)
    tps = MOE_TILES_PER_STEP if nt % MOE_TILES_PER_STEP == 0 else 1
    ts, spt = tps * tt, nt // tps
    sample = lambda bi, s: bi * ng + jnp.maximum(s - ne, 0) // spt
    tile = lambda s: jnp.maximum(s - ne, 0) % spt
    row = pl.BlockSpec((ng, 1, 1, l), lambda bi, s, c: (bi, ex(s), 0, 0))
    wspec = lambda k, n: pl.BlockSpec((1, k, n), lambda bi, s, c: (ex(s), 0, 0))
    grid_spec = pltpu.PrefetchScalarGridSpec(
        num_scalar_prefetch=1,
        grid=(b // ng, ne + ng * spt),
        in_specs=[pl.BlockSpec((ng, l, d), lambda bi, s, c: (bi, 0, 0),
                               pipeline_mode=pl.Buffered(1)),
                  row, row,
                  pl.BlockSpec((1, ne, ts), lambda bi, s, c: (sample(bi, s), 0, tile(s))),
                  wspec(d, f), wspec(d, f), wspec(f, d),
                  pl.BlockSpec((1, ts, d), lambda bi, s, c: (sample(bi, s), tile(s), 0)),
                  pl.BlockSpec((ng, N_MOD, d), lambda bi, s, c: (bi, 0, 0)),
                  pl.BlockSpec((1, d), lambda bi, s, c: (0, 0)),
                  pl.BlockSpec((ne, ne * win), lambda bi, s, c: (0, 0))],
        out_specs=pl.BlockSpec((1, ts, d), lambda bi, s, c: (sample(bi, s), tile(s), 0)),
        scratch_shapes=[pltpu.VMEM((ng, ne * cap + win, d), BF16),
                        pltpu.VMEM((ng, cap + MOE_GATHER_WINDOW, d), F32)])
    return pl.pallas_call(
        functools.partial(_moe_kernel, cap=cap, ne=ne, nt=nt),
        out_shape=jax.ShapeDtypeStruct((b, l, d), F32),
        grid_spec=grid_spec,
        compiler_params=_params("parallel", "arbitrary"),
        name="moe",
    )(cnt, hx, slot_row, aff_row, slot_t, wg, wu, wd, x1, mods, g_final.reshape(1, d), expand)


def kernel(x, c, ctx, c_ctx, w_ada, b_ada, g_norm1, w_in, b_gates, w_conv, w_fourier, w_mlstm,
           w_out, g_norm2, w_router, w_gate_e, w_up_e, w_down_e, g_final):
    assert w_ada.shape[0] == 1, "single-layer stack"
    b, l, d = x.shape
    lc = ctx.shape[1]
    d_f = d // 2
    d_m = d // 2
    dh = d_m // N_H
    ne = w_router.shape[2]
    cap = CAP_FACTOR * l // ne
    off_qk = d_f
    off_v = off_qk + 2 * d_m
    off_o = off_v + d_m
    off_g = off_o + d_m
    off_br = off_g + N_GATE_KINDS * N_H

    n_c = b + 1
    n_pad = -n_c % 8
    c_all = jnp.concatenate([c, c_ctx[None, :], jnp.zeros((n_pad, d), F32)], axis=0)
    mods = _ada(c_all, w_ada[0], b_ada[0]).reshape(n_c + n_pad, N_MOD, d)

    w = w_in[0]
    wb = w.astype(BF16)
    w_f, w_qk, w_v, w_o, w_br = (wb[:, 0:off_qk], wb[:, off_qk:off_v], wb[:, off_v:off_o],
                                 wb[:, off_o:off_g], wb[:, off_br:])
    w_gt = wb[:, off_g:off_br].T
    w9 = w_conv[0].reshape(9, 2 * d_m)
    qk_scale = jnp.concatenate([jnp.ones((1, d_m), F32), jnp.full((1, d_m), dh ** -0.5, F32)], axis=1)

    pc_qk, pc_v, gc_t = _inproj(ctx, mods, lambda bi, i: (b, 0, 0), g_norm1[0],
                                [w_qk, w_v], w_gt, tm=lc)
    q_c = _conv(pc_qk, w9, qk_scale, 1, 0, d_m, False)
    kt_c = _conv(pc_qk, w9, qk_scale, 1, d_m, 2 * d_m, True)
    rr_c = _gateprep(gc_t, b_gates[0], min(SCAN_CHUNK, lc))
    c0 = jnp.zeros((b, 2, N_H, dh, 2 * dh), F32)
    m0 = jnp.zeros((b, 2, N_H, 1, LANES), F32)
    c_seed, m_seed = _scan(q_c, kt_c, pc_v, rr_c, c0, m0, hps=N_H, want_h=False,
                           want_state=True)

    p_f, p_qk, p_v, g_t = _inproj(x, mods, lambda bi, i: (bi, 0, 0), g_norm1[0],
                                  [w_f, w_qk, w_v], w_gt, tm=1024)
    q_x = _conv(p_qk, w9, qk_scale, l // GRID_W, 0, d_m, False)
    kt_x = _conv(p_qk, w9, qk_scale, l // GRID_W, d_m, 2 * d_m, True)
    rr_x = _gateprep(g_t, b_gates[0], SCAN_CHUNK)
    (hm,) = _scan(q_x, kt_x, p_v, rr_x, c_seed, m_seed, hps=SCAN_HEADS_PER_STEP, want_h=True,
                  want_state=False)
    fu = _fourier(p_f)
    x1, hx2, aff_t = _merge(fu, hm, x, mods, g_norm1[0], w_br, w_o, w_fourier[0].astype(BF16),
                            w_mlstm[0].astype(BF16), w_out[0].astype(BF16), g_norm2[0], w_router[0],
                            tm=1024)

    slot_t, before_t = _topk(aff_t.reshape(b * ne, l), cap)
    return _moe(hx2, slot_t.reshape(b, ne, l), before_t.reshape(b, ne, l), aff_t,
                w_gate_e[0].astype(BF16), w_up_e[0].astype(BF16), w_down_e[0].astype(BF16),
                x1, mods, g_final, cap)
```

```python
import functools
import math

import jax
import jax.numpy as jnp
from jax import lax
from jax.experimental import pallas as pl
from jax.experimental.pallas import tpu as pltpu

F32 = jnp.float32
BF16 = jnp.bfloat16

GRID_W = 64
N_FG = 4
N_H = 4
N_GATE_KINDS = 4
N_EXPERTS = 16
CAP_FACTOR = 2
EPS = 1e-6
LOG2E = math.log2(math.e)
N_MOD = 6

LANES = 128
V7X_VMEM_BYTES = 64 * 1024 * 1024
VMEM_LIMIT = V7X_VMEM_BYTES - 8 * 1024 * 1024

SCAN_CHUNK = 256
SCAN_HEADS_PER_STEP = 1
N_SCAN_ROWS = 8
N_TIE_EXTRACT = 3
MERGE_SUBTILE = 256
MOE_GROUP = 2
MOE_TILE = 256
MOE_TILES_PER_STEP = 2
MOE_WINDOW = 64
MOE_GATHER_WINDOW = 64
ROW_ALIGN = 8
SLOT_ALIGN = 16


def _params(*sem):
    return pltpu.CompilerParams(dimension_semantics=sem, vmem_limit_bytes=VMEM_LIMIT)


def _silu(v):
    return v * jax.nn.sigmoid(v)


def _rms_mod(v, g, shift, scale):
    ms = jnp.mean(v * v, axis=-1, keepdims=True)
    return (v * lax.rsqrt(ms + EPS) * g) * (1.0 + scale) + shift


def _ada_kernel(c_ref, w_ref, b_ref, o_ref):
    s = _silu(c_ref[...])
    o_ref[...] = jnp.dot(s, w_ref[...], preferred_element_type=F32,
                         precision=lax.Precision.HIGHEST) + b_ref[...]


def _ada(c_all, w_ada, b_ada):
    n, d = c_all.shape
    cols = w_ada.shape[1]
    tn = 512
    return pl.pallas_call(
        _ada_kernel,
        out_shape=jax.ShapeDtypeStruct((n, cols), F32),
        grid=(cols // tn,),
        in_specs=[pl.BlockSpec((n, d), lambda j: (0, 0)),
                  pl.BlockSpec((d, tn), lambda j: (0, j)),
                  pl.BlockSpec((1, tn), lambda j: (0, j))],
        out_specs=pl.BlockSpec((n, tn), lambda j: (0, j)),
        compiler_params=_params("parallel"),
        name="ada",
    )(c_all, w_ada, b_ada.reshape(1, cols))


def _inproj_kernel(x_ref, mod_ref, g_ref, *refs, n_seg):
    w_refs = refs[:n_seg]
    wgt_ref = refs[n_seg]
    o_refs = refs[n_seg + 1:2 * n_seg + 1]
    gt_ref = refs[2 * n_seg + 1]
    h = _rms_mod(x_ref[0], g_ref[...], mod_ref[0, 0:1, :], mod_ref[0, 1:2, :])
    hb = h.astype(BF16)
    for w_ref, o_ref in zip(w_refs, o_refs):
        o_ref[0] = jnp.dot(hb, w_ref[...], preferred_element_type=F32).astype(o_ref.dtype)
    gt_ref[0] = lax.dot_general(wgt_ref[...], hb, (((1,), (1,)), ((), ())),
                                preferred_element_type=F32)


def _inproj(x, mods, mod_row, g, w_segs, w_gt, tm):
    b, l, d = x.shape
    tm = min(tm, l)
    n_seg = len(w_segs)
    ng = w_gt.shape[0]
    const2 = lambda bi, i: (0, 0)
    in_specs = [pl.BlockSpec((1, tm, d), lambda bi, i: (bi, i, 0)),
                pl.BlockSpec((1, N_MOD, d), mod_row),
                pl.BlockSpec((1, d), const2)]
    in_specs += [pl.BlockSpec(w.shape, const2, pipeline_mode=pl.Buffered(1)) for w in w_segs]
    in_specs += [pl.BlockSpec(w_gt.shape, const2)]
    out_shape = [jax.ShapeDtypeStruct((b, l, w.shape[1]), BF16) for w in w_segs]
    out_shape += [jax.ShapeDtypeStruct((b, ng, l), F32)]
    out_specs = [pl.BlockSpec((1, tm, w.shape[1]), lambda bi, i: (bi, i, 0)) for w in w_segs]
    out_specs += [pl.BlockSpec((1, ng, tm), lambda bi, i: (bi, 0, i))]
    return pl.pallas_call(
        functools.partial(_inproj_kernel, n_seg=n_seg),
        out_shape=out_shape,
        grid=(b, l // tm),
        in_specs=in_specs,
        out_specs=out_specs,
        compiler_params=_params("parallel", "parallel"),
        name="inproj",
    )(x, mods, g.reshape(1, d), *w_segs, w_gt)


def _conv_kernel(p_ref, w_ref, s_ref, o_ref, *, rows, width, transpose):
    v = p_ref[0].astype(F32)
    l, ct = v.shape
    grid = lambda a: a.reshape(rows, width, ct)
    v3, vl3, vr3 = grid(v), grid(pltpu.roll(v, 1, 0)), grid(pltpu.roll(v, l - 1, 0))
    col = lax.broadcasted_iota(jnp.int32, (1, width, ct), 1)
    w = w_ref[...]
    wl = [jnp.where(col == 0, 0.0, w[3 * dr:3 * dr + 1, :][None]) for dr in range(3)]
    wr = [jnp.where(col == width - 1, 0.0, w[3 * dr + 2:3 * dr + 3, :][None]) for dr in range(3)]

    def tap(dr):
        return vl3 * wl[dr] + v3 * w[3 * dr + 1:3 * dr + 2, :][None] + vr3 * wr[dr]

    acc = tap(1)
    if rows > 1:
        edge = jnp.zeros((1, width, ct), F32)
        acc = acc + jnp.concatenate([edge, tap(0)[:-1]], axis=0)
        acc = acc + jnp.concatenate([tap(2)[1:], edge], axis=0)
    acc = acc.reshape(l, ct)
    y = _silu(acc) * s_ref[...]
    o_ref[0] = (y.T if transpose else y).astype(o_ref.dtype)


def _conv(p, w9, scale, rows, col_lo, col_hi, transpose):
    b, l, _ = p.shape
    ct = 512
    j0 = col_lo // ct
    n = col_hi - col_lo
    if transpose:
        out_shape = jax.ShapeDtypeStruct((b, n, l), BF16)
        out_spec = pl.BlockSpec((1, ct, l), lambda bi, j: (bi, j, 0))
    else:
        out_shape = jax.ShapeDtypeStruct((b, l, n), BF16)
        out_spec = pl.BlockSpec((1, l, ct), lambda bi, j: (bi, 0, j))
    return pl.pallas_call(
        functools.partial(_conv_kernel, rows=rows, width=l // rows, transpose=transpose),
        out_shape=out_shape,
        grid=(b, n // ct),
        in_specs=[pl.BlockSpec((1, l, ct), lambda bi, j: (bi, 0, j + j0)),
                  pl.BlockSpec((9, ct), lambda bi, j: (0, j + j0)),
                  pl.BlockSpec((1, ct), lambda bi, j: (0, j + j0))],
        out_specs=out_spec,
        compiler_params=_params("parallel", "parallel"),
        name="conv_kt" if transpose else "conv_q",
    )(p, w9, scale)


def _log_sigmoid(v):
    return jnp.minimum(v, 0.0) - jnp.log1p(jnp.exp(-jnp.abs(v)))


def _chunk_scan(v, op, ident, chunk, reverse):
    l = v.shape[1]
    pos = lax.broadcasted_iota(jnp.int32, v.shape, 1) % chunk
    s = 1
    while s < chunk:
        if reverse:
            shifted = jnp.where(pos < chunk - s, pltpu.roll(v, l - s, 1), ident)
        else:
            shifted = jnp.where(pos >= s, pltpu.roll(v, s, 1), ident)
        v = op(v, shifted)
        s *= 2
    return v


def _gateprep_kernel(g_ref, b_ref, o_ref, *, chunk):
    g = g_ref[0] + b_ref[...]
    i_f = g[0:N_H] * LOG2E
    f_f = _log_sigmoid(g[N_H:2 * N_H]) * LOG2E
    i_b = g[2 * N_H:3 * N_H] * LOG2E
    f_b = _log_sigmoid(g[3 * N_H:4 * N_H]) * LOG2E
    neg = -jnp.inf
    cum_f = _chunk_scan(f_f, jnp.add, 0.0, chunk, False)
    br_f = i_f - cum_f
    pm_f = _chunk_scan(br_f, jnp.maximum, neg, chunk, False)
    cum_b = _chunk_scan(f_b, jnp.add, 0.0, chunk, True)
    br_b = i_b - cum_b
    pm_b = _chunk_scan(br_b, jnp.maximum, neg, chunk, True)
    zero = jnp.zeros_like(cum_f[0:1])
    for h in range(N_H):
        o_ref[0, h] = jnp.concatenate(
            [a[h:h + 1] for a in (cum_f, br_f, pm_f, cum_b, br_b, pm_b)] + [zero, zero], axis=0)


def _gateprep(g_t, b_gates, chunk):
    b, ng, l = g_t.shape
    return pl.pallas_call(
        functools.partial(_gateprep_kernel, chunk=chunk),
        out_shape=jax.ShapeDtypeStruct((b, N_H, N_SCAN_ROWS, l), F32),
        grid=(b,),
        in_specs=[pl.BlockSpec((1, ng, l), lambda bi: (bi, 0, 0)),
                  pl.BlockSpec((ng, 1), lambda bi: (0, 0))],
        out_specs=pl.BlockSpec((1, N_H, N_SCAN_ROWS, l), lambda bi: (bi, 0, 0, 0)),
        compiler_params=_params("parallel"),
        name="gateprep",
    )(g_t, b_gates.reshape(ng, 1))


def _scan_kernel(*refs, chunk, hps, want_h, want_state):
    q_ref, kt_ref, v_ref, rr_ref, c0_ref, m0_ref = refs[:6]
    pos = 6
    if want_h:
        h_ref = refs[pos]
        pos += 1
    if want_state:
        cs_ref, ms_ref = refs[pos:pos + 2]
        pos += 2
    vext_scr = refs[pos]
    if want_h:
        hf_scr, hb_scr = refs[pos + 1:pos + 3]

    l, dh = q_ref.shape[1], q_ref.shape[2] // hps
    nc = l // chunk
    rep = chunk // dh
    jj = lax.broadcasted_iota(jnp.int32, (chunk, chunk), 0)
    ss = lax.broadcasted_iota(jnp.int32, (chunk, chunk), 1)
    causal = (jj >= ss, jj <= ss)

    for hh in range(hps):
        hl = slice(hh * dh, (hh + 1) * dh)
        vext_scr[hh, :, 0:dh] = v_ref[0, :, hl]
        vext_scr[hh, :, dh:2 * dh] = jnp.ones((l, dh), BF16)
        cols = rr_ref[0, hh].T
        cst = [c0_ref[0, d, hh] for d in range(2)]
        mst = [m0_ref[0, d, hh][:, 0:1] for d in range(2)]

        for i in range(nc):
            for d in range(2):
                c = i if d == 0 else nc - 1 - i
                lo = c * chunk
                end = chunk - 1 if d == 0 else 0
                r0 = 3 * d
                qc = q_ref[0, lo:lo + chunk, hl]
                ktc = kt_ref[0, hl, lo:lo + chunk]
                vc = vext_scr[hh, lo:lo + chunk, :]
                cum_c = cols[lo:lo + chunk, r0:r0 + 1]
                pm_c = cols[lo:lo + chunk, r0 + 2:r0 + 3]
                br_r = rr_ref[0, hh, r0 + 1:r0 + 2, lo:lo + chunk]
                tot = rr_ref[0, hh, r0:r0 + 1, lo + end:lo + end + 1]
                pm_end = rr_ref[0, hh, r0 + 2:r0 + 3, lo + end:lo + end + 1]
                m0 = mst[d]
                mm = jnp.broadcast_to(jnp.maximum(m0, pm_c), (chunk, dh))
                d_in = jnp.where(causal[d],
                                 jnp.exp2(br_r - jnp.concatenate([mm] * rep, axis=1)), 0.0)
                d_st = jnp.exp2(m0 - mm)
                s = jnp.dot(qc, ktc, preferred_element_type=F32)
                p = (s * d_in).astype(BF16)
                qd = (qc.astype(F32) * d_st).astype(BF16)
                both = (jnp.dot(p, vc, preferred_element_type=F32)
                        + jnp.dot(qd, cst[d].astype(BF16), preferred_element_type=F32))
                if want_h:
                    m_j = jnp.broadcast_to(cum_c, (chunk, dh)) + mm
                    hc = both[:, 0:dh] / jnp.maximum(jnp.abs(both[:, dh:2 * dh]), jnp.exp2(-m_j))
                    (hf_scr if d == 0 else hb_scr)[hh, lo:lo + chunk, :] = hc
                kw = (ktc.astype(F32) * d_in[end:end + 1, :]).astype(BF16)
                upd = jnp.dot(kw, vc, preferred_element_type=F32)
                dn = d_st[end:end + 1, :]
                cst[d] = jnp.concatenate([dn, dn], axis=1) * cst[d] + upd
                mst[d] = tot + jnp.maximum(m0, pm_end)

        if want_h:
            h_ref[0, :, hl] = (hf_scr[hh] + hb_scr[hh]).astype(h_ref.dtype)
        if want_state:
            for d in range(2):
                cs_ref[0, d, hh] = cst[d]
                ms_ref[0, d, hh] = jnp.broadcast_to(mst[d], (1, LANES))


def _scan(q, kt, v, rr, c0, m0, *, hps, want_h, want_state):
    b, l, dm = v.shape
    dh = dm // N_H
    chunk = min(SCAN_CHUNK, l)
    assert chunk % dh == 0 and l % chunk == 0 and N_H % hps == 0
    tok = pl.BlockSpec((1, l, hps * dh), lambda bi, h: (bi, 0, h))
    in_specs = [tok, pl.BlockSpec((1, hps * dh, l), lambda bi, h: (bi, h, 0)), tok]
    args = [q, kt, v]
    st_c = pl.BlockSpec((1, 2, hps, dh, 2 * dh), lambda bi, h: (bi, 0, h, 0, 0))
    st_m = pl.BlockSpec((1, 2, hps, 1, LANES), lambda bi, h: (bi, 0, h, 0, 0))
    in_specs += [pl.BlockSpec((1, hps, N_SCAN_ROWS, l), lambda bi, h: (bi, h, 0, 0)), st_c, st_m]
    args += [rr, c0, m0]
    out_shape, out_specs = [], []
    scratch = [pltpu.VMEM((hps, l, 2 * dh), BF16)]
    if want_h:
        out_shape.append(jax.ShapeDtypeStruct((b, l, dm), BF16))
        out_specs.append(tok)
        scratch += [pltpu.VMEM((hps, l, dh), F32), pltpu.VMEM((hps, l, dh), F32)]
    if want_state:
        out_shape += [jax.ShapeDtypeStruct(c0.shape, F32), jax.ShapeDtypeStruct(m0.shape, F32)]
        out_specs += [st_c, st_m]
    return pl.pallas_call(
        functools.partial(_scan_kernel, chunk=chunk, hps=hps, want_h=want_h, want_state=want_state),
        out_shape=out_shape,
        grid=(b, N_H // hps),
        in_specs=in_specs,
        out_specs=out_specs,
        scratch_shapes=scratch,
        compiler_params=_params("parallel", "parallel"),
        name="scan_h" if want_h else "scan_state",
    )(*args)


def _twiddle(z, k, n):
    zr, zi = z
    k %= n
    if k == 0:
        return z
    if 4 * k == n:
        return zi, -zr
    if 2 * k == n:
        return -zr, -zi
    if 4 * k == 3 * n:
        return -zi, zr
    wr, wi = math.cos(2.0 * math.pi * k / n), -math.sin(2.0 * math.pi * k / n)
    return zr * wr - zi * wi, zr * wi + zi * wr


def _fft_list(xs):
    n = len(xs)
    if n == 1:
        return xs
    ev, od = _fft_list(xs[0::2]), _fft_list(xs[1::2])
    out = [None] * n
    for k in range(n // 2):
        tr, ti = _twiddle(od[k], k, n)
        out[k] = (ev[k][0] + tr, ev[k][1] + ti)
        out[k + n // 2] = (ev[k][0] - tr, ev[k][1] - ti)
    return out


def _fourier_kernel(u_ref, wc_ref, tw_ref, cs_ref, o_ref, zz_scr, r_scr, *, fg, n1):
    l, df = u_ref.shape[1], u_ref.shape[2]
    n2 = l // n1
    for g in range(df // fg):
        r = jnp.dot(u_ref[0, :, g * fg:(g + 1) * fg], wc_ref[...], preferred_element_type=F32)
        xs = [(r[j * n2:(j + 1) * n2, 0:fg], r[j * n2:(j + 1) * n2, fg:2 * fg]) for j in range(n1)]
        zs = _fft_list(xs)
        for k1 in range(n1):
            zr, zi = zs[k1]
            if k1 > 0:
                tc, ts = tw_ref[0, k1], tw_ref[1, k1]
                zr, zi = zr * tc + zi * ts, zi * tc - zr * ts
            zz_scr[k1, 0:n2, g * fg:(g + 1) * fg] = zr.astype(BF16)
            zz_scr[k1, n2:2 * n2, g * fg:(g + 1) * fg] = zi.astype(BF16)
    pitch = r_scr.shape[1] // n1
    for k1 in range(n1):
        res = jnp.dot(cs_ref[...], zz_scr[k1], preferred_element_type=F32)
        for c in range(df // LANES):
            r_scr[c, k1 * pitch:k1 * pitch + n2, :] = res[:, c * LANES:(c + 1) * LANES]

    def emit(j, carry):
        for c in range(df // LANES):
            a = r_scr[c, pl.ds(2 * j, n1, stride=pitch), :]
            b = r_scr[c, pl.ds(2 * j + 1, n1, stride=pitch), :]
            o_ref[0, pl.ds(pl.multiple_of(2 * n1 * j, 2 * n1), 2 * n1), c * LANES:(c + 1) * LANES] = (
                jnp.concatenate([a, b], axis=0).astype(o_ref.dtype))
        return carry

    lax.fori_loop(0, n2 // 2, emit, 0, unroll=True)


def _dft_angles(rows, cols, n):
    ang = ((rows[:, None] * cols[None, :]) % n).astype(F32) * (2.0 * math.pi / n)
    return jnp.cos(ang), jnp.sin(ang)


def _fourier(u):
    b, l, df = u.shape
    fg = df // N_FG
    n1 = 8
    n2 = l // n1
    ar = lambda n: jnp.arange(n, dtype=jnp.int32)
    cc, sc = _dft_angles(ar(fg), ar(fg), fg)
    wc = (jnp.concatenate([cc, -sc], axis=1) * fg ** -0.5).astype(BF16)
    tc, ts = _dft_angles(ar(n1), ar(n2), l)
    tw = jnp.broadcast_to(jnp.stack([tc, ts])[..., None], (2, n1, n2, fg))
    c2, s2 = _dft_angles(ar(n2), ar(n2), n2)
    cs = (jnp.concatenate([c2, s2], axis=1) * l ** -0.5).astype(BF16)
    const = lambda a: pl.BlockSpec(a.shape, lambda bi: (0,) * a.ndim, pipeline_mode=pl.Buffered(1))
    pitch = n2 + 4
    return pl.pallas_call(
        functools.partial(_fourier_kernel, fg=fg, n1=n1),
        out_shape=jax.ShapeDtypeStruct((b, l, df), BF16),
        grid=(b,),
        in_specs=[pl.BlockSpec((1, l, df), lambda bi: (bi, 0, 0)), const(wc), const(tw), const(cs)],
        out_specs=pl.BlockSpec((1, l, df), lambda bi: (bi, 0, 0)),
        scratch_shapes=[pltpu.VMEM((n1, 2 * n2, df), BF16),
                        pltpu.VMEM((df // LANES, n1 * pitch, LANES), F32)],
        compiler_params=_params("parallel"),
        name="fourier",
    )(u, wc, tw, cs)


def _merge_kernel(fu_ref, hm_ref, x_ref, mod_ref, g1_ref, wbr_ref, wog_ref, wf_ref, wm_ref, wo_ref,
                  g2_ref, wr1_ref, wr2_ref, x1_ref, hx_ref, aff_ref):
    tm, d = x_ref.shape[1], x_ref.shape[2]
    ne = aff_ref.shape[1]
    for r0 in range(0, tm, MERGE_SUBTILE):
        rows = slice(r0, r0 + MERGE_SUBTILE)
        x = x_ref[0, rows, :]
        h1 = _rms_mod(x, g1_ref[...], mod_ref[0, 0:1, :], mod_ref[0, 1:2, :]).astype(BF16)
        br = jnp.dot(h1, wbr_ref[...], preferred_element_type=F32)
        og = jnp.dot(h1, wog_ref[...], preferred_element_type=F32)
        yf = jnp.dot(fu_ref[0, rows, :], wf_ref[...], preferred_element_type=F32)
        hm = (jax.nn.sigmoid(og) * hm_ref[0, rows, :].astype(F32)).astype(BF16)
        ym = jnp.dot(hm, wm_ref[...], preferred_element_type=F32)
        merged = (jax.nn.sigmoid(br[:, 0:d]) * yf + jax.nn.sigmoid(br[:, d:2 * d]) * ym).astype(BF16)
        o = jnp.dot(merged, wo_ref[...], preferred_element_type=F32)
        x1 = x + mod_ref[0, 2:3, :] * o
        x1_ref[0, rows, :] = x1
        h2 = _rms_mod(x1, g2_ref[...], mod_ref[0, 3:4, :], mod_ref[0, 4:5, :])
        hi = h2.astype(BF16)
        hx_ref[0, rows, :] = hi
        lo = (h2 - hi.astype(F32)).astype(BF16)
        r = (jnp.dot(hi, wr1_ref[...], preferred_element_type=F32)
             + jnp.dot(lo, wr2_ref[...], preferred_element_type=F32))
        rt = r.T
        logits = rt[0:ne, :] + rt[ne:2 * ne, :]
        e = jnp.exp(logits - jnp.max(logits, axis=0, keepdims=True))
        aff_ref[0, :, rows] = e / jnp.sum(e, axis=0, keepdims=True)


def _merge(fu, hm, x, mods, g1, w_br, w_og, wf, wm, wo, g2, w_router, tm):
    b, l, d = x.shape
    tm = min(tm, l)
    ne = w_router.shape[1]
    whi = w_router.astype(BF16)
    wlo = (w_router - whi.astype(F32)).astype(BF16)
    pad = jnp.zeros((d, LANES - 2 * ne), BF16)
    wr1 = jnp.concatenate([whi, wlo, pad], axis=1)
    wr2 = jnp.concatenate([whi, jnp.zeros((d, ne), BF16), pad], axis=1)
    tok = lambda n: pl.BlockSpec((1, tm, n), lambda bi, i: (bi, i, 0))
    const = lambda a: pl.BlockSpec(a.shape, lambda bi, i: (0, 0), pipeline_mode=pl.Buffered(1))
    vec = pl.BlockSpec((1, d), lambda bi, i: (0, 0))
    return pl.pallas_call(
        _merge_kernel,
        out_shape=[jax.ShapeDtypeStruct((b, l, d), F32),
                   jax.ShapeDtypeStruct((b, l, d), BF16),
                   jax.ShapeDtypeStruct((b, ne, l), F32)],
        grid=(b, l // tm),
        in_specs=[tok(fu.shape[2]), tok(hm.shape[2]), tok(d),
                  pl.BlockSpec((1, N_MOD, d), lambda bi, i: (bi, 0, 0)),
                  vec, const(w_br), const(w_og), const(wf), const(wm), const(wo), vec,
                  const(wr1), const(wr2)],
        out_specs=[tok(d), tok(d), pl.BlockSpec((1, ne, tm), lambda bi, i: (bi, 0, i))],
        compiler_params=_params("parallel", "parallel"),
        name="merge",
    )(fu, hm, x, mods, g1.reshape(1, d), w_br, w_og, wf, wm, wo, g2.reshape(1, d), wr1, wr2)


def _topk_kernel(a_ref, u_ref, slot_ref, before_ref, *, k):
    a = a_ref[...]
    l = a.shape[1]
    total = lambda m: jnp.sum(m, axis=1, keepdims=True)
    ge = lambda bits: jnp.where(a >= pltpu.bitcast(bits, F32), 1.0, 0.0)

    def body(i, t):
        cand = t | lax.shift_left(jnp.int32(1), 30 - i)
        return jnp.where(total(ge(cand)) >= k, cand, t)

    t = lax.fori_loop(0, 31, body, jnp.zeros((a.shape[0], 1), jnp.int32))
    chosen = ge(t + 1)
    rem = ge(t) - chosen
    need = k - total(chosen)
    idx = lax.broadcasted_iota(jnp.int32, a.shape, 1).astype(F32)
    for _ in range(N_TIE_EXTRACT):
        cur = jnp.where(rem > 0.0, a, -1.0)
        is_mx = jnp.where(cur == jnp.max(cur, axis=1, keepdims=True), rem, 0.0)
        first = jnp.min(jnp.where(is_mx > 0.0, idx, float(l)), axis=1, keepdims=True)
        pick = jnp.where(idx == first, jnp.where(need > 0.0, 1.0, 0.0), 0.0)
        chosen = chosen + pick
        rem = rem - pick
        need = need - total(pick)
    before = jnp.dot(rem.astype(BF16), u_ref[...], preferred_element_type=F32)
    sel = chosen + jnp.where(before < need, rem, 0.0)
    rank = jnp.dot(sel.astype(BF16), u_ref[...], preferred_element_type=F32)
    slot_ref[...] = jnp.where(sel > 0.0, rank, -1.0)
    before_ref[...] = rank


def _topk(aff_t, k):
    r, l = aff_t.shape
    tr = min(128, r)
    idx = jnp.arange(l, dtype=jnp.int32)
    upper = (idx[:, None] < idx[None, :]).astype(BF16)
    return pl.pallas_call(
        functools.partial(_topk_kernel, k=k),
        out_shape=[jax.ShapeDtypeStruct((r, l), F32)] * 2,
        grid=(r // tr,),
        in_specs=[pl.BlockSpec((tr, l), lambda i: (i, 0)),
                  pl.BlockSpec((l, l), lambda i: (0, 0), pipeline_mode=pl.Buffered(1))],
        out_specs=[pl.BlockSpec((tr, l), lambda i: (i, 0))] * 2,
        compiler_params=_params("parallel"),
        name="topk",
    )(aff_t, upper)


def _moe_kernel(cnt_ref, hx_ref, srow_ref, arow_ref, slot_ref, wg_ref, wu_ref, wd_ref, x1_ref,
                mod_ref, g_ref, xp_ref, o_ref, y_scr, xs_scr, *, cap, ne, nt):
    bi = pl.program_id(0)
    s = pl.program_id(1)
    ng = hx_ref.shape[0]
    win = MOE_WINDOW
    per_sample = ne * (nt + 1)

    @pl.when(s == 0)
    def _():
        for gi in range(ng):
            y_scr[gi, ne * cap:ne * cap + win, :] = jnp.zeros((win, y_scr.shape[2]), BF16)

    @pl.when(s < ne)
    def _():
        l = hx_ref.shape[1]
        tb, gw = MOE_TILE, MOE_GATHER_WINDOW
        xs_scr[...] = jnp.zeros(xs_scr.shape, F32)
        row_i = lax.broadcasted_iota(jnp.int32, (gw, tb), 0)
        lows, n_pass = [], 0
        for gi in range(ng):
            base = (bi * ng + gi) * per_sample + s * (nt + 1)
            for j in range(nt):
                low = (cnt_ref[base + j] // ROW_ALIGN) * ROW_ALIGN
                lows.append(low)
                n_pass = jnp.maximum(n_pass, (cnt_ref[base + j + 1] - low + gw - 1) // gw)

        def gather_pass(w, carry):
            for gi in range(ng):
                for j in range(nt):
                    lo = pl.multiple_of(jnp.minimum(lows[gi * nt + j] + w * gw, cap), ROW_ALIGN)
                    srow_blk = srow_ref[gi, 0, :, j * tb:(j + 1) * tb]
                    p = jnp.where(srow_blk == (row_i + lo).astype(F32), 1.0, 0.0).astype(BF16)
                    xs_scr[gi, pl.ds(lo, gw), :] += jnp.dot(
                        p, hx_ref[gi, j * tb:(j + 1) * tb, :], preferred_element_type=F32)
            return carry

        gather_pass(0, 0)
        lax.fori_loop(1, n_pass, gather_pass, 0)
        slot_i = lax.broadcasted_iota(jnp.int32, (cap, l), 0).astype(F32)
        for gi in range(ng):
            xs = xs_scr[gi, 0:cap, :].astype(BF16)
            gate = jnp.dot(xs, wg_ref[0], preferred_element_type=F32)
            up = jnp.dot(xs, wu_ref[0], preferred_element_type=F32)
            hid = (_silu(gate) * up).astype(BF16)
            y = jnp.dot(hid, wd_ref[0], preferred_element_type=F32)
            hit = srow_ref[gi, 0] == slot_i
            vals = jnp.sum(jnp.where(hit, arow_ref[gi, 0], 0.0), axis=1, keepdims=True)
            y_scr[gi, pl.ds(pl.multiple_of(s * cap, cap), cap), :] = (y * vals).astype(BF16)

    @pl.when(s >= ne)
    def _():
        tt, d = MOE_TILE, x1_ref.shape[2]
        tps = x1_ref.shape[1] // tt
        gi = (s - ne) // (nt // tps)
        j0 = ((s - ne) % (nt // tps)) * tps
        lane = lax.broadcasted_iota(jnp.int32, (1, ne * win), 1)
        grp, off = lane // win, lane % win
        slots = slot_ref[0].T.astype(BF16)
        tiles, n_pass = [], 0
        for u in range(tps):
            base = (bi * ng + gi) * per_sample + j0 + u
            lows = []
            for e in range(ne):
                start = cnt_ref[base + e * (nt + 1)]
                end = cnt_ref[base + e * (nt + 1) + 1]
                low = (start // SLOT_ALIGN) * SLOT_ALIGN
                lows.append(low)
                n_pass = jnp.maximum(n_pass, (end - low + win - 1) // win)
            sexp = jnp.dot(slots[u * tt:(u + 1) * tt, :], xp_ref[...], preferred_element_type=F32)
            tiles.append((lows, sexp))

        def one_pass(w, accs):
            out = []
            for (lows, sexp), acc in zip(tiles, accs):
                tgt, rows = off, []
                for e in range(ne):
                    lo = jnp.minimum(lows[e] + w * win, cap)
                    tgt = jnp.where(grp == e, off + lo, tgt)
                    rows.append(
                        y_scr[gi, pl.ds(pl.multiple_of(e * cap + lo, SLOT_ALIGN), win), :])
                pt = jnp.where(sexp == tgt.astype(F32), 1.0, 0.0).astype(BF16)
                out.append(acc + jnp.dot(pt, jnp.concatenate(rows, axis=0),
                                         preferred_element_type=F32))
            return tuple(out)

        def finish(extra_passes):
            accs = one_pass(0, tuple(jnp.zeros((tt, d), F32) for _ in range(tps)))
            if extra_passes:
                accs = lax.fori_loop(1, n_pass, one_pass, accs)
            for u, moe in enumerate(accs):
                v = x1_ref[0, u * tt:(u + 1) * tt, :] + mod_ref[gi, 5:6, :] * moe
                ms = jnp.mean(v * v, axis=-1, keepdims=True)
                o_ref[0, u * tt:(u + 1) * tt, :] = v * lax.rsqrt(ms + EPS) * g_ref[...]

        pl.when(n_pass <= 1)(lambda: finish(False))
        pl.when(n_pass > 1)(lambda: finish(True))


def _moe(hx, slot_t, before_t, aff_t, wg, wu, wd, x1, mods, g_final, cap):
    b, l, d = hx.shape
    ne, _, f = wg.shape
    tt = MOE_TILE
    nt = l // tt
    win = MOE_WINDOW
    ng = MOE_GROUP if b % MOE_GROUP == 0 else 1
    assert cap % SLOT_ALIGN == 0 and win % SLOT_ALIGN == 0 and (win & (win - 1)) == 0
    slot_row = slot_t.reshape(b, ne, 1, l)
    aff_row = aff_t.reshape(b, ne, 1, l)
    cnt = jnp.concatenate([before_t[:, :, ::tt], jnp.full((b, ne, 1), cap, F32)], axis=2)
    cnt = cnt.astype(jnp.int32).reshape(b * ne * (nt + 1))
    lane = jnp.arange(ne * win, dtype=jnp.int32)
    expand = (lane[None, :] // win == jnp.arange(ne, dtype=jnp.int32)[:, None]).astype(BF16)
    ex = lambda s: jnp.minimum(s, ne - 1)
    tps = MOE_TILES_PER_STEP if nt % MOE_TILES_PER_STEP == 0 else 1
    ts, spt = tps * tt, nt // tps
    sample = lambda bi, s: bi * ng + jnp.maximum(s - ne, 0) // spt
    tile = lambda s: jnp.maximum(s - ne, 0) % spt
    row = pl.BlockSpec((ng, 1, 1, l), lambda bi, s, c: (bi, ex(s), 0, 0))
    wspec = lambda k, n: pl.BlockSpec((1, k, n), lambda bi, s, c: (ex(s), 0, 0))
    grid_spec = pltpu.PrefetchScalarGridSpec(
        num_scalar_prefetch=1,
        grid=(b // ng, ne + ng * spt),
        in_specs=[pl.BlockSpec((ng, l, d), lambda bi, s, c: (bi, 0, 0),
                               pipeline_mode=pl.Buffered(1)),
                  row, row,
                  pl.BlockSpec((1, ne, ts), lambda bi, s, c: (sample(bi, s), 0, tile(s))),
                  wspec(d, f), wspec(d, f), wspec(f, d),
                  pl.BlockSpec((1, ts, d), lambda bi, s, c: (sample(bi, s), tile(s), 0)),
                  pl.BlockSpec((ng, N_MOD, d), lambda bi, s, c: (bi, 0, 0)),
                  pl.BlockSpec((1, d), lambda bi, s, c: (0, 0)),
                  pl.BlockSpec((ne, ne * win), lambda bi, s, c: (0, 0))],
        out_specs=pl.BlockSpec((1, ts, d), lambda bi, s, c: (sample(bi, s), tile(s), 0)),
        scratch_shapes=[pltpu.VMEM((ng, ne * cap + win, d), BF16),
                        pltpu.VMEM((ng, cap + MOE_GATHER_WINDOW, d), F32)])
    return pl.pallas_call(
        functools.partial(_moe_kernel, cap=cap, ne=ne, nt=nt),
        out_shape=jax.ShapeDtypeStruct((b, l, d), F32),
        grid_spec=grid_spec,
        compiler_params=_params("parallel", "arbitrary"),
        name="moe",
    )(cnt, hx, slot_row, aff_row, slot_t, wg, wu, wd, x1, mods, g_final.reshape(1, d), expand)


def kernel(x, c, ctx, c_ctx, w_ada, b_ada, g_norm1, w_in, b_gates, w_conv, w_fourier, w_mlstm,
           w_out, g_norm2, w_router, w_gate_e, w_up_e, w_down_e, g_final):
    assert w_ada.shape[0] == 1, "single-layer stack"
    b, l, d = x.shape
    lc = ctx.shape[1]
    d_f = d // 2
    d_m = d // 2
    dh = d_m // N_H
    ne = w_router.shape[2]
    cap = CAP_FACTOR * l // ne
    off_qk = d_f
    off_v = off_qk + 2 * d_m
    off_o = off_v + d_m
    off_g = off_o + d_m
    off_br = off_g + N_GATE_KINDS * N_H

    n_c = b + 1
    n_pad = -n_c % 8
    c_all = jnp.concatenate([c, c_ctx[None, :], jnp.zeros((n_pad, d), F32)], axis=0)
    mods = _ada(c_all, w_ada[0], b_ada[0]).reshape(n_c + n_pad, N_MOD, d)

    w = w_in[0]
    wb = w.astype(BF16)
    w_f, w_qk, w_v, w_o, w_br = (wb[:, 0:off_qk], wb[:, off_qk:off_v], wb[:, off_v:off_o],
                                 wb[:, off_o:off_g], wb[:, off_br:])
    w_gt = wb[:, off_g:off_br].T
    w9 = w_conv[0].reshape(9, 2 * d_m)
    qk_scale = jnp.concatenate([jnp.ones((1, d_m), F32), jnp.full((1, d_m), dh ** -0.5, F32)], axis=1)

    pc_qk, pc_v, gc_t = _inproj(ctx, mods, lambda bi, i: (b, 0, 0), g_norm1[0],
                                [w_qk, w_v], w_gt, tm=lc)
    q_c = _conv(pc_qk, w9, qk_scale, 1, 0, d_m, False)
    kt_c = _conv(pc_qk, w9, qk_scale, 1, d_m, 2 * d_m, True)
    rr_c = _gateprep(gc_t, b_gates[0], min(SCAN_CHUNK, lc))
    c0 = jnp.zeros((b, 2, N_H, dh, 2 * dh), F32)
    m0 = jnp.zeros((b, 2, N_H, 1, LANES), F32)
    c_seed, m_seed = _scan(q_c, kt_c, pc_v, rr_c, c0, m0, hps=N_H, want_h=False,
                           want_state=True)

    p_f, p_qk, p_v, g_t = _inproj(x, mods, lambda bi, i: (bi, 0, 0), g_norm1[0],
                                  [w_f, w_qk, w_v], w_gt, tm=1024)
    q_x = _conv(p_qk, w9, qk_scale, l // GRID_W, 0, d_m, False)
    kt_x = _conv(p_qk, w9, qk_scale, l // GRID_W, d_m, 2 * d_m, True)
    rr_x = _gateprep(g_t, b_gates[0], SCAN_CHUNK)
    (hm,) = _scan(q_x, kt_x, p_v, rr_x, c_seed, m_seed, hps=SCAN_HEADS_PER_STEP, want_h=True,
                  want_state=False)
    fu = _fourier(p_f)
    x1, hx2, aff_t = _merge(fu, hm, x, mods, g_norm1[0], w_br, w_o, w_fourier[0].astype(BF16),
                            w_mlstm[0].astype(BF16), w_out[0].astype(BF16), g_norm2[0], w_router[0],
                            tm=1024)

    slot_t, before_t = _topk(aff_t.reshape(b * ne, l), cap)
    return _moe(hx2, slot_t.reshape(b, ne, l), before_t.reshape(b, ne, l), aff_t,
                w_gate_e[0].astype(BF16), w_up_e[0].astype(BF16), w_down_e[0].astype(BF16),
                x1, mods, g_final, cap)
```

```python
import functools
import math

import jax
import jax.numpy as jnp
from jax import lax
from jax.experimental import pallas as pl
from jax.experimental.pallas import tpu as pltpu

F32 = jnp.float32
BF16 = jnp.bfloat16

GRID_W = 64
N_FG = 4
N_H = 4
N_GATE_KINDS = 4
N_EXPERTS = 16
CAP_FACTOR = 2
EPS = 1e-6
LOG2E = math.log2(math.e)
N_MOD = 6

LANES = 128
V7X_VMEM_BYTES = 64 * 1024 * 1024
VMEM_LIMIT = V7X_VMEM_BYTES - 8 * 1024 * 1024

SCAN_CHUNK = 256
SCAN_HEADS_PER_STEP = 1
N_SCAN_ROWS = 8
N_TIE_EXTRACT = 3
MERGE_SUBTILE = 256
MOE_GROUP = 2
MOE_TILE = 256
MOE_TILES_PER_STEP = 2
MOE_WINDOW = 64
MOE_GATHER_WINDOW = 128
MOE_GATHER_TILES = 2
ROW_ALIGN = 8
SLOT_ALIGN = 16


def _params(*sem):
    return pltpu.CompilerParams(dimension_semantics=sem, vmem_limit_bytes=VMEM_LIMIT)


def _silu(v):
    return v * jax.nn.sigmoid(v)


def _rms_mod(v, g, shift, scale):
    ms = jnp.mean(v * v, axis=-1, keepdims=True)
    return (v * lax.rsqrt(ms + EPS) * g) * (1.0 + scale) + shift


def _ada_kernel(c_ref, w_ref, b_ref, o_ref):
    s = _silu(c_ref[...])
    o_ref[...] = jnp.dot(s, w_ref[...], preferred_element_type=F32,
                         precision=lax.Precision.HIGHEST) + b_ref[...]


def _ada(c_all, w_ada, b_ada):
    n, d = c_all.shape
    cols = w_ada.shape[1]
    tn = 512
    return pl.pallas_call(
        _ada_kernel,
        out_shape=jax.ShapeDtypeStruct((n, cols), F32),
        grid=(cols // tn,),
        in_specs=[pl.BlockSpec((n, d), lambda j: (0, 0)),
                  pl.BlockSpec((d, tn), lambda j: (0, j)),
                  pl.BlockSpec((1, tn), lambda j: (0, j))],
        out_specs=pl.BlockSpec((n, tn), lambda j: (0, j)),
        compiler_params=_params("parallel"),
        name="ada",
    )(c_all, w_ada, b_ada.reshape(1, cols))


def _inproj_kernel(x_ref, mod_ref, g_ref, *refs, n_seg):
    w_refs = refs[:n_seg]
    wgt_ref = refs[n_seg]
    o_refs = refs[n_seg + 1:2 * n_seg + 1]
    gt_ref = refs[2 * n_seg + 1]
    h = _rms_mod(x_ref[0], g_ref[...], mod_ref[0, 0:1, :], mod_ref[0, 1:2, :])
    hb = h.astype(BF16)
    for w_ref, o_ref in zip(w_refs, o_refs):
        o_ref[0] = jnp.dot(hb, w_ref[...], preferred_element_type=F32).astype(o_ref.dtype)
    gt_ref[0] = lax.dot_general(wgt_ref[...], hb, (((1,), (1,)), ((), ())),
                                preferred_element_type=F32)


def _inproj(x, mods, mod_row, g, w_segs, w_gt, tm):
    b, l, d = x.shape
    tm = min(tm, l)
    n_seg = len(w_segs)
    ng = w_gt.shape[0]
    const2 = lambda bi, i: (0, 0)
    in_specs = [pl.BlockSpec((1, tm, d), lambda bi, i: (bi, i, 0)),
                pl.BlockSpec((1, N_MOD, d), mod_row),
                pl.BlockSpec((1, d), const2)]
    in_specs += [pl.BlockSpec(w.shape, const2, pipeline_mode=pl.Buffered(1)) for w in w_segs]
    in_specs += [pl.BlockSpec(w_gt.shape, const2)]
    out_shape = [jax.ShapeDtypeStruct((b, l, w.shape[1]), BF16) for w in w_segs]
    out_shape += [jax.ShapeDtypeStruct((b, ng, l), F32)]
    out_specs = [pl.BlockSpec((1, tm, w.shape[1]), lambda bi, i: (bi, i, 0)) for w in w_segs]
    out_specs += [pl.BlockSpec((1, ng, tm), lambda bi, i: (bi, 0, i))]
    return pl.pallas_call(
        functools.partial(_inproj_kernel, n_seg=n_seg),
        out_shape=out_shape,
        grid=(b, l // tm),
        in_specs=in_specs,
        out_specs=out_specs,
        compiler_params=_params("parallel", "parallel"),
        name="inproj",
    )(x, mods, g.reshape(1, d), *w_segs, w_gt)


def _conv_kernel(p_ref, w_ref, s_ref, o_ref, *, rows, width, transpose):
    v = p_ref[0].astype(F32)
    l, ct = v.shape
    grid = lambda a: a.reshape(rows, width, ct)
    v3, vl3, vr3 = grid(v), grid(pltpu.roll(v, 1, 0)), grid(pltpu.roll(v, l - 1, 0))
    col = lax.broadcasted_iota(jnp.int32, (1, width, ct), 1)
    w = w_ref[...]
    wl = [jnp.where(col == 0, 0.0, w[3 * dr:3 * dr + 1, :][None]) for dr in range(3)]
    wr = [jnp.where(col == width - 1, 0.0, w[3 * dr + 2:3 * dr + 3, :][None]) for dr in range(3)]

    def tap(dr):
        return vl3 * wl[dr] + v3 * w[3 * dr + 1:3 * dr + 2, :][None] + vr3 * wr[dr]

    acc = tap(1)
    if rows > 1:
        edge = jnp.zeros((1, width, ct), F32)
        acc = acc + jnp.concatenate([edge, tap(0)[:-1]], axis=0)
        acc = acc + jnp.concatenate([tap(2)[1:], edge], axis=0)
    acc = acc.reshape(l, ct)
    y = _silu(acc) * s_ref[...]
    o_ref[0] = (y.T if transpose else y).astype(o_ref.dtype)


def _conv(p, w9, scale, rows, col_lo, col_hi, transpose):
    b, l, _ = p.shape
    ct = 512
    j0 = col_lo // ct
    n = col_hi - col_lo
    if transpose:
        out_shape = jax.ShapeDtypeStruct((b, n, l), BF16)
        out_spec = pl.BlockSpec((1, ct, l), lambda bi, j: (bi, j, 0))
    else:
        out_shape = jax.ShapeDtypeStruct((b, l, n), BF16)
        out_spec = pl.BlockSpec((1, l, ct), lambda bi, j: (bi, 0, j))
    return pl.pallas_call(
        functools.partial(_conv_kernel, rows=rows, width=l // rows, transpose=transpose),
        out_shape=out_shape,
        grid=(b, n // ct),
        in_specs=[pl.BlockSpec((1, l, ct), lambda bi, j: (bi, 0, j + j0)),
                  pl.BlockSpec((9, ct), lambda bi, j: (0, j + j0)),
                  pl.BlockSpec((1, ct), lambda bi, j: (0, j + j0))],
        out_specs=out_spec,
        compiler_params=_params("parallel", "parallel"),
        name="conv_kt" if transpose else "conv_q",
    )(p, w9, scale)


def _log_sigmoid(v):
    return jnp.minimum(v, 0.0) - jnp.log1p(jnp.exp(-jnp.abs(v)))


def _chunk_scan(v, op, ident, chunk, reverse):
    l = v.shape[1]
    pos = lax.broadcasted_iota(jnp.int32, v.shape, 1) % chunk
    s = 1
    while s < chunk:
        if reverse:
            shifted = jnp.where(pos < chunk - s, pltpu.roll(v, l - s, 1), ident)
        else:
            shifted = jnp.where(pos >= s, pltpu.roll(v, s, 1), ident)
        v = op(v, shifted)
        s *= 2
    return v


def _gateprep_kernel(g_ref, b_ref, o_ref, *, chunk):
    g = g_ref[0] + b_ref[...]
    i_f = g[0:N_H] * LOG2E
    f_f = _log_sigmoid(g[N_H:2 * N_H]) * LOG2E
    i_b = g[2 * N_H:3 * N_H] * LOG2E
    f_b = _log_sigmoid(g[3 * N_H:4 * N_H]) * LOG2E
    neg = -jnp.inf
    cum_f = _chunk_scan(f_f, jnp.add, 0.0, chunk, False)
    br_f = i_f - cum_f
    pm_f = _chunk_scan(br_f, jnp.maximum, neg, chunk, False)
    cum_b = _chunk_scan(f_b, jnp.add, 0.0, chunk, True)
    br_b = i_b - cum_b
    pm_b = _chunk_scan(br_b, jnp.maximum, neg, chunk, True)
    zero = jnp.zeros_like(cum_f[0:1])
    for h in range(N_H):
        o_ref[0, h] = jnp.concatenate(
            [a[h:h + 1] for a in (cum_f, br_f, pm_f, cum_b, br_b, pm_b)] + [zero, zero], axis=0)


def _gateprep(g_t, b_gates, chunk):
    b, ng, l = g_t.shape
    return pl.pallas_call(
        functools.partial(_gateprep_kernel, chunk=chunk),
        out_shape=jax.ShapeDtypeStruct((b, N_H, N_SCAN_ROWS, l), F32),
        grid=(b,),
        in_specs=[pl.BlockSpec((1, ng, l), lambda bi: (bi, 0, 0)),
                  pl.BlockSpec((ng, 1), lambda bi: (0, 0))],
        out_specs=pl.BlockSpec((1, N_H, N_SCAN_ROWS, l), lambda bi: (bi, 0, 0, 0)),
        compiler_params=_params("parallel"),
        name="gateprep",
    )(g_t, b_gates.reshape(ng, 1))


def _scan_kernel(*refs, chunk, hps, want_h, want_state):
    q_ref, kt_ref, v_ref, rr_ref, c0_ref, m0_ref = refs[:6]
    pos = 6
    if want_h:
        h_ref = refs[pos]
        pos += 1
    if want_state:
        cs_ref, ms_ref = refs[pos:pos + 2]
        pos += 2
    vext_scr = refs[pos]
    if want_h:
        hf_scr, hb_scr = refs[pos + 1:pos + 3]

    l, dh = q_ref.shape[1], q_ref.shape[2] // hps
    nc = l // chunk
    rep = chunk // dh
    jj = lax.broadcasted_iota(jnp.int32, (chunk, chunk), 0)
    ss = lax.broadcasted_iota(jnp.int32, (chunk, chunk), 1)
    causal = (jj >= ss, jj <= ss)

    for hh in range(hps):
        hl = slice(hh * dh, (hh + 1) * dh)
        vext_scr[hh, :, 0:dh] = v_ref[0, :, hl]
        vext_scr[hh, :, dh:2 * dh] = jnp.ones((l, dh), BF16)
        cols = rr_ref[0, hh].T
        cst = [c0_ref[0, d, hh] for d in range(2)]
        mst = [m0_ref[0, d, hh][:, 0:1] for d in range(2)]

        for i in range(nc):
            for d in range(2):
                c = i if d == 0 else nc - 1 - i
                lo = c * chunk
                end = chunk - 1 if d == 0 else 0
                r0 = 3 * d
                qc = q_ref[0, lo:lo + chunk, hl]
                ktc = kt_ref[0, hl, lo:lo + chunk]
                vc = vext_scr[hh, lo:lo + chunk, :]
                cum_c = cols[lo:lo + chunk, r0:r0 + 1]
                pm_c = cols[lo:lo + chunk, r0 + 2:r0 + 3]
                br_r = rr_ref[0, hh, r0 + 1:r0 + 2, lo:lo + chunk]
                tot = rr_ref[0, hh, r0:r0 + 1, lo + end:lo + end + 1]
                pm_end = rr_ref[0, hh, r0 + 2:r0 + 3, lo + end:lo + end + 1]
                m0 = mst[d]
                mm = jnp.broadcast_to(jnp.maximum(m0, pm_c), (chunk, dh))
                d_in = jnp.where(causal[d],
                                 jnp.exp2(br_r - jnp.concatenate([mm] * rep, axis=1)), 0.0)
                d_st = jnp.exp2(m0 - mm)
                s = jnp.dot(qc, ktc, preferred_element_type=F32)
                p = (s * d_in).astype(BF16)
                qd = (qc.astype(F32) * d_st).astype(BF16)
                both = (jnp.dot(p, vc, preferred_element_type=F32)
                        + jnp.dot(qd, cst[d].astype(BF16), preferred_element_type=F32))
                if want_h:
                    m_j = jnp.broadcast_to(cum_c, (chunk, dh)) + mm
                    hc = both[:, 0:dh] / jnp.maximum(jnp.abs(both[:, dh:2 * dh]), jnp.exp2(-m_j))
                    (hf_scr if d == 0 else hb_scr)[hh, lo:lo + chunk, :] = hc
                kw = (ktc.astype(F32) * d_in[end:end + 1, :]).astype(BF16)
                upd = jnp.dot(kw, vc, preferred_element_type=F32)
                dn = d_st[end:end + 1, :]
                cst[d] = jnp.concatenate([dn, dn], axis=1) * cst[d] + upd
                mst[d] = tot + jnp.maximum(m0, pm_end)

        if want_h:
            h_ref[0, :, hl] = (hf_scr[hh] + hb_scr[hh]).astype(h_ref.dtype)
        if want_state:
            for d in range(2):
                cs_ref[0, d, hh] = cst[d]
                ms_ref[0, d, hh] = jnp.broadcast_to(mst[d], (1, LANES))


def _scan(q, kt, v, rr, c0, m0, *, hps, want_h, want_state):
    b, l, dm = v.shape
    dh = dm // N_H
    chunk = min(SCAN_CHUNK, l)
    assert chunk % dh == 0 and l % chunk == 0 and N_H % hps == 0
    tok = pl.BlockSpec((1, l, hps * dh), lambda bi, h: (bi, 0, h))
    in_specs = [tok, pl.BlockSpec((1, hps * dh, l), lambda bi, h: (bi, h, 0)), tok]
    args = [q, kt, v]
    st_c = pl.BlockSpec((1, 2, hps, dh, 2 * dh), lambda bi, h: (bi, 0, h, 0, 0))
    st_m = pl.BlockSpec((1, 2, hps, 1, LANES), lambda bi, h: (bi, 0, h, 0, 0))
    in_specs += [pl.BlockSpec((1, hps, N_SCAN_ROWS, l), lambda bi, h: (bi, h, 0, 0)), st_c, st_m]
    args += [rr, c0, m0]
    out_shape, out_specs = [], []
    scratch = [pltpu.VMEM((hps, l, 2 * dh), BF16)]
    if want_h:
        out_shape.append(jax.ShapeDtypeStruct((b, l, dm), BF16))
        out_specs.append(tok)
        scratch += [pltpu.VMEM((hps, l, dh), F32), pltpu.VMEM((hps, l, dh), F32)]
    if want_state:
        out_shape += [jax.ShapeDtypeStruct(c0.shape, F32), jax.ShapeDtypeStruct(m0.shape, F32)]
        out_specs += [st_c, st_m]
    return pl.pallas_call(
        functools.partial(_scan_kernel, chunk=chunk, hps=hps, want_h=want_h, want_state=want_state),
        out_shape=out_shape,
        grid=(b, N_H // hps),
        in_specs=in_specs,
        out_specs=out_specs,
        scratch_shapes=scratch,
        compiler_params=_params("parallel", "parallel"),
        name="scan_h" if want_h else "scan_state",
    )(*args)


def _twiddle(z, k, n):
    zr, zi = z
    k %= n
    if k == 0:
        return z
    if 4 * k == n:
        return zi, -zr
    if 2 * k == n:
        return -zr, -zi
    if 4 * k == 3 * n:
        return -zi, zr
    wr, wi = math.cos(2.0 * math.pi * k / n), -math.sin(2.0 * math.pi * k / n)
    return zr * wr - zi * wi, zr * wi + zi * wr


def _fft_list(xs):
    n = len(xs)
    if n == 1:
        return xs
    ev, od = _fft_list(xs[0::2]), _fft_list(xs[1::2])
    out = [None] * n
    for k in range(n // 2):
        tr, ti = _twiddle(od[k], k, n)
        out[k] = (ev[k][0] + tr, ev[k][1] + ti)
        out[k + n // 2] = (ev[k][0] - tr, ev[k][1] - ti)
    return out


def _fourier_kernel(u_ref, wc_ref, tw_ref, cs_ref, o_ref, zz_scr, r_scr, *, fg, n1):
    l, df = u_ref.shape[1], u_ref.shape[2]
    n2 = l // n1
    for g in range(df // fg):
        r = jnp.dot(u_ref[0, :, g * fg:(g + 1) * fg], wc_ref[...], preferred_element_type=F32)
        xs = [(r[j * n2:(j + 1) * n2, 0:fg], r[j * n2:(j + 1) * n2, fg:2 * fg]) for j in range(n1)]
        zs = _fft_list(xs)
        for k1 in range(n1):
            zr, zi = zs[k1]
            if k1 > 0:
                tc, ts = tw_ref[0, k1], tw_ref[1, k1]
                zr, zi = zr * tc + zi * ts, zi * tc - zr * ts
            zz_scr[k1, 0:n2, g * fg:(g + 1) * fg] = zr.astype(BF16)
            zz_scr[k1, n2:2 * n2, g * fg:(g + 1) * fg] = zi.astype(BF16)
    pitch = r_scr.shape[1] // n1
    for k1 in range(n1):
        res = jnp.dot(cs_ref[...], zz_scr[k1], preferred_element_type=F32)
        for c in range(df // LANES):
            r_scr[c, k1 * pitch:k1 * pitch + n2, :] = res[:, c * LANES:(c + 1) * LANES]

    def emit(j, carry):
        for c in range(df // LANES):
            a = r_scr[c, pl.ds(2 * j, n1, stride=pitch), :]
            b = r_scr[c, pl.ds(2 * j + 1, n1, stride=pitch), :]
            o_ref[0, pl.ds(pl.multiple_of(2 * n1 * j, 2 * n1), 2 * n1), c * LANES:(c + 1) * LANES] = (
                jnp.concatenate([a, b], axis=0).astype(o_ref.dtype))
        return carry

    lax.fori_loop(0, n2 // 2, emit, 0, unroll=True)


def _dft_angles(rows, cols, n):
    ang = ((rows[:, None] * cols[None, :]) % n).astype(F32) * (2.0 * math.pi / n)
    return jnp.cos(ang), jnp.sin(ang)


def _fourier(u):
    b, l, df = u.shape
    fg = df // N_FG
    n1 = 8
    n2 = l // n1
    ar = lambda n: jnp.arange(n, dtype=jnp.int32)
    cc, sc = _dft_angles(ar(fg), ar(fg), fg)
    wc = (jnp.concatenate([cc, -sc], axis=1) * fg ** -0.5).astype(BF16)
    tc, ts = _dft_angles(ar(n1), ar(n2), l)
    tw = jnp.broadcast_to(jnp.stack([tc, ts])[..., None], (2, n1, n2, fg))
    c2, s2 = _dft_angles(ar(n2), ar(n2), n2)
    cs = (jnp.concatenate([c2, s2], axis=1) * l ** -0.5).astype(BF16)
    const = lambda a: pl.BlockSpec(a.shape, lambda bi: (0,) * a.ndim, pipeline_mode=pl.Buffered(1))
    pitch = n2 + 4
    return pl.pallas_call(
        functools.partial(_fourier_kernel, fg=fg, n1=n1),
        out_shape=jax.ShapeDtypeStruct((b, l, df), BF16),
        grid=(b,),
        in_specs=[pl.BlockSpec((1, l, df), lambda bi: (bi, 0, 0)), const(wc), const(tw), const(cs)],
        out_specs=pl.BlockSpec((1, l, df), lambda bi: (bi, 0, 0)),
        scratch_shapes=[pltpu.VMEM((n1, 2 * n2, df), BF16),
                        pltpu.VMEM((df // LANES, n1 * pitch, LANES), F32)],
        compiler_params=_params("parallel"),
        name="fourier",
    )(u, wc, tw, cs)


def _merge_kernel(fu_ref, hm_ref, x_ref, mod_ref, g1_ref, wbr_ref, wog_ref, wf_ref, wm_ref, wo_ref,
                  g2_ref, wr1_ref, wr2_ref, x1_ref, hx_ref, aff_ref):
    tm, d = x_ref.shape[1], x_ref.shape[2]
    ne = aff_ref.shape[1]
    for r0 in range(0, tm, MERGE_SUBTILE):
        rows = slice(r0, r0 + MERGE_SUBTILE)
        x = x_ref[0, rows, :]
        h1 = _rms_mod(x, g1_ref[...], mod_ref[0, 0:1, :], mod_ref[0, 1:2, :]).astype(BF16)
        br = jnp.dot(h1, wbr_ref[...], preferred_element_type=F32)
        og = jnp.dot(h1, wog_ref[...], preferred_element_type=F32)
        yf = jnp.dot(fu_ref[0, rows, :], wf_ref[...], preferred_element_type=F32)
        hm = (jax.nn.sigmoid(og) * hm_ref[0, rows, :].astype(F32)).astype(BF16)
        ym = jnp.dot(hm, wm_ref[...], preferred_element_type=F32)
        merged = (jax.nn.sigmoid(br[:, 0:d]) * yf + jax.nn.sigmoid(br[:, d:2 * d]) * ym).astype(BF16)
        o = jnp.dot(merged, wo_ref[...], preferred_element_type=F32)
        x1 = x + mod_ref[0, 2:3, :] * o
        x1_ref[0, rows, :] = x1
        h2 = _rms_mod(x1, g2_ref[...], mod_ref[0, 3:4, :], mod_ref[0, 4:5, :])
        hi = h2.astype(BF16)
        hx_ref[0, rows, :] = hi
        lo = (h2 - hi.astype(F32)).astype(BF16)
        r = (jnp.dot(hi, wr1_ref[...], preferred_element_type=F32)
             + jnp.dot(lo, wr2_ref[...], preferred_element_type=F32))
        rt = r.T
        logits = rt[0:ne, :] + rt[ne:2 * ne, :]
        e = jnp.exp(logits - jnp.max(logits, axis=0, keepdims=True))
        aff_ref[0, :, rows] = e / jnp.sum(e, axis=0, keepdims=True)


def _merge(fu, hm, x, mods, g1, w_br, w_og, wf, wm, wo, g2, w_router, tm):
    b, l, d = x.shape
    tm = min(tm, l)
    ne = w_router.shape[1]
    whi = w_router.astype(BF16)
    wlo = (w_router - whi.astype(F32)).astype(BF16)
    pad = jnp.zeros((d, LANES - 2 * ne), BF16)
    wr1 = jnp.concatenate([whi, wlo, pad], axis=1)
    wr2 = jnp.concatenate([whi, jnp.zeros((d, ne), BF16), pad], axis=1)
    tok = lambda n: pl.BlockSpec((1, tm, n), lambda bi, i: (bi, i, 0))
    const = lambda a: pl.BlockSpec(a.shape, lambda bi, i: (0, 0), pipeline_mode=pl.Buffered(1))
    vec = pl.BlockSpec((1, d), lambda bi, i: (0, 0))
    return pl.pallas_call(
        _merge_kernel,
        out_shape=[jax.ShapeDtypeStruct((b, l, d), F32),
                   jax.ShapeDtypeStruct((b, l, d), BF16),
                   jax.ShapeDtypeStruct((b, ne, l), F32)],
        grid=(b, l // tm),
        in_specs=[tok(fu.shape[2]), tok(hm.shape[2]), tok(d),
                  pl.BlockSpec((1, N_MOD, d), lambda bi, i: (bi, 0, 0)),
                  vec, const(w_br), const(w_og), const(wf), const(wm), const(wo), vec,
                  const(wr1), const(wr2)],
        out_specs=[tok(d), tok(d), pl.BlockSpec((1, ne, tm), lambda bi, i: (bi, 0, i))],
        compiler_params=_params("parallel", "parallel"),
        name="merge",
    )(fu, hm, x, mods, g1.reshape(1, d), w_br, w_og, wf, wm, wo, g2.reshape(1, d), wr1, wr2)


def _topk_kernel(a_ref, u_ref, slot_ref, before_ref, *, k):
    a = a_ref[...]
    l = a.shape[1]
    total = lambda m: jnp.sum(m, axis=1, keepdims=True)
    ge = lambda bits: jnp.where(a >= pltpu.bitcast(bits, F32), 1.0, 0.0)

    def body(i, t):
        cand = t | lax.shift_left(jnp.int32(1), 30 - i)
        return jnp.where(total(ge(cand)) >= k, cand, t)

    t = lax.fori_loop(0, 31, body, jnp.zeros((a.shape[0], 1), jnp.int32))
    chosen = ge(t + 1)
    rem = ge(t) - chosen
    need = k - total(chosen)
    idx = lax.broadcasted_iota(jnp.int32, a.shape, 1).astype(F32)
    for _ in range(N_TIE_EXTRACT):
        cur = jnp.where(rem > 0.0, a, -1.0)
        is_mx = jnp.where(cur == jnp.max(cur, axis=1, keepdims=True), rem, 0.0)
        first = jnp.min(jnp.where(is_mx > 0.0, idx, float(l)), axis=1, keepdims=True)
        pick = jnp.where(idx == first, jnp.where(need > 0.0, 1.0, 0.0), 0.0)
        chosen = chosen + pick
        rem = rem - pick
        need = need - total(pick)
    before = jnp.dot(rem.astype(BF16), u_ref[...], preferred_element_type=F32)
    sel = chosen + jnp.where(before < need, rem, 0.0)
    rank = jnp.dot(sel.astype(BF16), u_ref[...], preferred_element_type=F32)
    slot_ref[...] = jnp.where(sel > 0.0, rank, -1.0)
    before_ref[...] = rank


def _topk(aff_t, k):
    r, l = aff_t.shape
    tr = min(128, r)
    idx = jnp.arange(l, dtype=jnp.int32)
    upper = (idx[:, None] < idx[None, :]).astype(BF16)
    return pl.pallas_call(
        functools.partial(_topk_kernel, k=k),
        out_shape=[jax.ShapeDtypeStruct((r, l), F32)] * 2,
        grid=(r // tr,),
        in_specs=[pl.BlockSpec((tr, l), lambda i: (i, 0)),
                  pl.BlockSpec((l, l), lambda i: (0, 0), pipeline_mode=pl.Buffered(1))],
        out_specs=[pl.BlockSpec((tr, l), lambda i: (i, 0))] * 2,
        compiler_params=_params("parallel"),
        name="topk",
    )(aff_t, upper)


def _moe_kernel(cnt_ref, hx_ref, srow_ref, arow_ref, slot_ref, wg_ref, wu_ref, wd_ref, x1_ref,
                mod_ref, g_ref, xp_ref, o_ref, y_scr, xs_scr, *, cap, ne, nt):
    bi = pl.program_id(0)
    s = pl.program_id(1)
    ng = hx_ref.shape[0]
    win = MOE_WINDOW
    per_sample = ne * (nt + 1)

    @pl.when(s == 0)
    def _():
        for gi in range(ng):
            y_scr[gi, ne * cap:ne * cap + win, :] = jnp.zeros((win, y_scr.shape[2]), BF16)

    @pl.when(s < ne)
    def _():
        l = hx_ref.shape[1]
        gw = MOE_GATHER_WINDOW
        tpb = MOE_GATHER_TILES if nt % MOE_GATHER_TILES == 0 else 1
        tb, nb = tpb * MOE_TILE, nt // tpb
        xs_scr[...] = jnp.zeros(xs_scr.shape, F32)
        row_i = lax.broadcasted_iota(jnp.int32, (gw, tb), 0)
        lows, n_pass = [], 0
        for gi in range(ng):
            base = (bi * ng + gi) * per_sample + s * (nt + 1)
            for j in range(nb):
                low = (cnt_ref[base + j * tpb] // ROW_ALIGN) * ROW_ALIGN
                lows.append(low)
                n_pass = jnp.maximum(n_pass, (cnt_ref[base + (j + 1) * tpb] - low + gw - 1) // gw)

        def gather_pass(w, carry):
            for gi in range(ng):
                for j in range(nb):
                    lo = pl.multiple_of(jnp.minimum(lows[gi * nb + j] + w * gw, cap), ROW_ALIGN)
                    srow_blk = srow_ref[gi, 0, :, j * tb:(j + 1) * tb]
                    p = jnp.where(srow_blk == (row_i + lo).astype(F32), 1.0, 0.0).astype(BF16)
                    xs_scr[gi, pl.ds(lo, gw), :] += jnp.dot(
                        p, hx_ref[gi, j * tb:(j + 1) * tb, :], preferred_element_type=F32)
            return carry

        gather_pass(0, 0)
        lax.fori_loop(1, n_pass, gather_pass, 0)
        slot_i = lax.broadcasted_iota(jnp.int32, (cap, l), 0).astype(F32)
        for gi in range(ng):
            xs = xs_scr[gi, 0:cap, :].astype(BF16)
            gate = jnp.dot(xs, wg_ref[0], preferred_element_type=F32)
            up = jnp.dot(xs, wu_ref[0], preferred_element_type=F32)
            hid = (_silu(gate) * up).astype(BF16)
            y = jnp.dot(hid, wd_ref[0], preferred_element_type=F32)
            hit = srow_ref[gi, 0] == slot_i
            vals = jnp.sum(jnp.where(hit, arow_ref[gi, 0], 0.0), axis=1, keepdims=True)
            y_scr[gi, pl.ds(pl.multiple_of(s * cap, cap), cap), :] = (y * vals).astype(BF16)

    @pl.when(s >= ne)
    def _():
        tt, d = MOE_TILE, x1_ref.shape[2]
        tps = x1_ref.shape[1] // tt
        gi = (s - ne) // (nt // tps)
        j0 = ((s - ne) % (nt // tps)) * tps
        lane = lax.broadcasted_iota(jnp.int32, (1, ne * win), 1)
        grp, off = lane // win, lane % win
        slots = slot_ref[0].T.astype(BF16)
        tiles, n_pass = [], 0
        for u in range(tps):
            base = (bi * ng + gi) * per_sample + j0 + u
            lows = []
            for e in range(ne):
                start = cnt_ref[base + e * (nt + 1)]
                end = cnt_ref[base + e * (nt + 1) + 1]
                low = (start // SLOT_ALIGN) * SLOT_ALIGN
                lows.append(low)
                n_pass = jnp.maximum(n_pass, (end - low + win - 1) // win)
            sexp = jnp.dot(slots[u * tt:(u + 1) * tt, :], xp_ref[...], preferred_element_type=F32)
            tiles.append((lows, sexp))

        def one_pass(w, accs):
            out = []
            for (lows, sexp), acc in zip(tiles, accs):
                tgt, rows = off, []
                for e in range(ne):
                    lo = jnp.minimum(lows[e] + w * win, cap)
                    tgt = jnp.where(grp == e, off + lo, tgt)
                    rows.append(
                        y_scr[gi, pl.ds(pl.multiple_of(e * cap + lo, SLOT_ALIGN), win), :])
                pt = jnp.where(sexp == tgt.astype(F32), 1.0, 0.0).astype(BF16)
                out.append(acc + jnp.dot(pt, jnp.concatenate(rows, axis=0),
                                         preferred_element_type=F32))
            return tuple(out)

        def finish(extra_passes):
            accs = one_pass(0, tuple(jnp.zeros((tt, d), F32) for _ in range(tps)))
            if extra_passes:
                accs = lax.fori_loop(1, n_pass, one_pass, accs)
            for u, moe in enumerate(accs):
                v = x1_ref[0, u * tt:(u + 1) * tt, :] + mod_ref[gi, 5:6, :] * moe
                ms = jnp.mean(v * v, axis=-1, keepdims=True)
                o_ref[0, u * tt:(u + 1) * tt, :] = v * lax.rsqrt(ms + EPS) * g_ref[...]

        pl.when(n_pass <= 1)(lambda: finish(False))
        pl.when(n_pass > 1)(lambda: finish(True))


def _moe(hx, slot_t, before_t, aff_t, wg, wu, wd, x1, mods, g_final, cap):
    b, l, d = hx.shape
    ne, _, f = wg.shape
    tt = MOE_TILE
    nt = l // tt
    win = MOE_WINDOW
    ng = MOE_GROUP if b % MOE_GROUP == 0 else 1
    assert cap % SLOT_ALIGN == 0 and win % SLOT_ALIGN == 0 and (win & (win - 1)) == 0
    slot_row = slot_t.reshape(b, ne, 1, l)
    aff_row = aff_t.reshape(b, ne, 1, l)
    cnt = jnp.concatenate([before_t[:, :, ::tt], jnp.full((b, ne, 1), cap, F32)], axis=2)
    cnt = cnt.astype(jnp.int32).reshape(b * ne * (nt + 1))
    lane = jnp.arange(ne * win, dtype=jnp.int32)
    expand = (lane[None, :] // win == jnp.arange(ne, dtype=jnp.int32)[:, None]).astype(BF16)
    ex = lambda s: jnp.minimum(s, ne - 1)
    tps = MOE_TILES_PER_STEP if nt % MOE_TILES_PER_STEP == 0 else 1
    ts, spt = tps * tt, nt // tps
    sample = lambda bi, s: bi * ng + jnp.maximum(s - ne, 0) // spt
    tile = lambda s: jnp.maximum(s - ne, 0) % spt
    row = pl.BlockSpec((ng, 1, 1, l), lambda bi, s, c: (bi, ex(s), 0, 0))
    wspec = lambda k, n: pl.BlockSpec((1, k, n), lambda bi, s, c: (ex(s), 0, 0))
    grid_spec = pltpu.PrefetchScalarGridSpec(
        num_scalar_prefetch=1,
        grid=(b // ng, ne + ng * spt),
        in_specs=[pl.BlockSpec((ng, l, d), lambda bi, s, c: (bi, 0, 0),
                               pipeline_mode=pl.Buffered(1)),
                  row, row,
                  pl.BlockSpec((1, ne, ts), lambda bi, s, c: (sample(bi, s), 0, tile(s))),
                  wspec(d, f), wspec(d, f), wspec(f, d),
                  pl.BlockSpec((1, ts, d), lambda bi, s, c: (sample(bi, s), tile(s), 0)),
                  pl.BlockSpec((ng, N_MOD, d), lambda bi, s, c: (bi, 0, 0)),
                  pl.BlockSpec((1, d), lambda bi, s, c: (0, 0)),
                  pl.BlockSpec((ne, ne * win), lambda bi, s, c: (0, 0))],
        out_specs=pl.BlockSpec((1, ts, d), lambda bi, s, c: (sample(bi, s), tile(s), 0)),
        scratch_shapes=[pltpu.VMEM((ng, ne * cap + win, d), BF16),
                        pltpu.VMEM((ng, cap + MOE_GATHER_WINDOW, d), F32)])
    return pl.pallas_call(
        functools.partial(_moe_kernel, cap=cap, ne=ne, nt=nt),
        out_shape=jax.ShapeDtypeStruct((b, l, d), F32),
        grid_spec=grid_spec,
        compiler_params=_params("parallel", "arbitrary"),
        name="moe",
    )(cnt, hx, slot_row, aff_row, slot_t, wg, wu, wd, x1, mods, g_final.reshape(1, d), expand)


def kernel(x, c, ctx, c_ctx, w_ada, b_ada, g_norm1, w_in, b_gates, w_conv, w_fourier, w_mlstm,
           w_out, g_norm2, w_router, w_gate_e, w_up_e, w_down_e, g_final):
    assert w_ada.shape[0] == 1, "single-layer stack"
    b, l, d = x.shape
    lc = ctx.shape[1]
    d_f = d // 2
    d_m = d // 2
    dh = d_m // N_H
    ne = w_router.shape[2]
    cap = CAP_FACTOR * l // ne
    off_qk = d_f
    off_v = off_qk + 2 * d_m
    off_o = off_v + d_m
    off_g = off_o + d_m
    off_br = off_g + N_GATE_KINDS * N_H

    n_c = b + 1
    n_pad = -n_c % 8
    c_all = jnp.concatenate([c, c_ctx[None, :], jnp.zeros((n_pad, d), F32)], axis=0)
    mods = _ada(c_all, w_ada[0], b_ada[0]).reshape(n_c + n_pad, N_MOD, d)

    w = w_in[0]
    wb = w.astype(BF16)
    w_f, w_qk, w_v, w_o, w_br = (wb[:, 0:off_qk], wb[:, off_qk:off_v], wb[:, off_v:off_o],
                                 wb[:, off_o:off_g], wb[:, off_br:])
    w_gt = wb[:, off_g:off_br].T
    w9 = w_conv[0].reshape(9, 2 * d_m)
    qk_scale = jnp.concatenate([jnp.ones((1, d_m), F32), jnp.full((1, d_m), dh ** -0.5, F32)], axis=1)

    pc_qk, pc_v, gc_t = _inproj(ctx, mods, lambda bi, i: (b, 0, 0), g_norm1[0],
                                [w_qk, w_v], w_gt, tm=lc)
    q_c = _conv(pc_qk, w9, qk_scale, 1, 0, d_m, False)
    kt_c = _conv(pc_qk, w9, qk_scale, 1, d_m, 2 * d_m, True)
    rr_c = _gateprep(gc_t, b_gates[0], min(SCAN_CHUNK, lc))
    c0 = jnp.zeros((b, 2, N_H, dh, 2 * dh), F32)
    m0 = jnp.zeros((b, 2, N_H, 1, LANES), F32)
    c_seed, m_seed = _scan(q_c, kt_c, pc_v, rr_c, c0, m0, hps=N_H, want_h=False,
                           want_state=True)

    p_f, p_qk, p_v, g_t = _inproj(x, mods, lambda bi, i: (bi, 0, 0), g_norm1[0],
                                  [w_f, w_qk, w_v], w_gt, tm=1024)
    q_x = _conv(p_qk, w9, qk_scale, l // GRID_W, 0, d_m, False)
    kt_x = _conv(p_qk, w9, qk_scale, l // GRID_W, d_m, 2 * d_m, True)
    rr_x = _gateprep(g_t, b_gates[0], SCAN_CHUNK)
    (hm,) = _scan(q_x, kt_x, p_v, rr_x, c_seed, m_seed, hps=SCAN_HEADS_PER_STEP, want_h=True,
                  want_state=False)
    fu = _fourier(p_f)
    x1, hx2, aff_t = _merge(fu, hm, x, mods, g_norm1[0], w_br, w_o, w_fourier[0].astype(BF16),
                            w_mlstm[0].astype(BF16), w_out[0].astype(BF16), g_norm2[0], w_router[0],
                            tm=1024)

    slot_t, before_t = _topk(aff_t.reshape(b * ne, l), cap)
    return _moe(hx2, slot_t.reshape(b, ne, l), before_t.reshape(b, ne, l), aff_t,
                w_gate_e[0].astype(BF16), w_up_e[0].astype(BF16), w_down_e[0].astype(BF16),
                x1, mods, g_final, cap)
```

```python
import functools
import math

import jax
import jax.numpy as jnp
from jax import lax
from jax.experimental import pallas as pl
from jax.experimental.pallas import tpu as pltpu

F32 = jnp.float32
BF16 = jnp.bfloat16

GRID_W = 64
N_FG = 4
N_H = 4
N_GATE_KINDS = 4
N_EXPERTS = 16
CAP_FACTOR = 2
EPS = 1e-6
LOG2E = math.log2(math.e)
N_MOD = 6

LANES = 128
V7X_VMEM_BYTES = 64 * 1024 * 1024
VMEM_LIMIT = V7X_VMEM_BYTES - 8 * 1024 * 1024

SCAN_CHUNK = 256
SCAN_HEADS_PER_STEP = 1
N_SCAN_ROWS = 8
N_TIE_EXTRACT = 3
MERGE_SUBTILE = 256
MOE_GROUP = 2
MOE_TILE = 256
MOE_TILES_PER_STEP = 2
MOE_WINDOW = 64
MOE_GATHER_WINDOW = 128
MOE_GATHER_TILES = 2
ROW_ALIGN = 8
SLOT_ALIGN = 16


def _params(*sem):
    return pltpu.CompilerParams(dimension_semantics=sem, vmem_limit_bytes=VMEM_LIMIT)


def _silu(v):
    return v * jax.nn.sigmoid(v)


def _rms_mod(v, g, shift, scale):
    ms = jnp.mean(v * v, axis=-1, keepdims=True)
    return (v * lax.rsqrt(ms + EPS) * g) * (1.0 + scale) + shift


def _ada_kernel(c_ref, w_ref, b_ref, o_ref):
    s = _silu(c_ref[...])
    o_ref[...] = jnp.dot(s, w_ref[...], preferred_element_type=F32,
                         precision=lax.Precision.HIGHEST) + b_ref[...]


def _ada(c_all, w_ada, b_ada):
    n, d = c_all.shape
    cols = w_ada.shape[1]
    tn = 512
    return pl.pallas_call(
        _ada_kernel,
        out_shape=jax.ShapeDtypeStruct((n, cols), F32),
        grid=(cols // tn,),
        in_specs=[pl.BlockSpec((n, d), lambda j: (0, 0)),
                  pl.BlockSpec((d, tn), lambda j: (0, j)),
                  pl.BlockSpec((1, tn), lambda j: (0, j))],
        out_specs=pl.BlockSpec((n, tn), lambda j: (0, j)),
        compiler_params=_params("parallel"),
        name="ada",
    )(c_all, w_ada, b_ada.reshape(1, cols))


def _inproj_kernel(x_ref, mod_ref, g_ref, *refs, n_seg):
    w_refs = refs[:n_seg]
    wgt_ref = refs[n_seg]
    o_refs = refs[n_seg + 1:2 * n_seg + 1]
    gt_ref = refs[2 * n_seg + 1]
    h = _rms_mod(x_ref[0], g_ref[...], mod_ref[0, 0:1, :], mod_ref[0, 1:2, :])
    hb = h.astype(BF16)
    for w_ref, o_ref in zip(w_refs, o_refs):
        o_ref[0] = jnp.dot(hb, w_ref[...], preferred_element_type=F32).astype(o_ref.dtype)
    gt_ref[0] = lax.dot_general(wgt_ref[...], hb, (((1,), (1,)), ((), ())),
                                preferred_element_type=F32)


def _inproj(x, mods, mod_row, g, w_segs, w_gt, tm):
    b, l, d = x.shape
    tm = min(tm, l)
    n_seg = len(w_segs)
    ng = w_gt.shape[0]
    const2 = lambda bi, i: (0, 0)
    in_specs = [pl.BlockSpec((1, tm, d), lambda bi, i: (bi, i, 0)),
                pl.BlockSpec((1, N_MOD, d), mod_row),
                pl.BlockSpec((1, d), const2)]
    in_specs += [pl.BlockSpec(w.shape, const2, pipeline_mode=pl.Buffered(1)) for w in w_segs]
    in_specs += [pl.BlockSpec(w_gt.shape, const2)]
    out_shape = [jax.ShapeDtypeStruct((b, l, w.shape[1]), BF16) for w in w_segs]
    out_shape += [jax.ShapeDtypeStruct((b, ng, l), F32)]
    out_specs = [pl.BlockSpec((1, tm, w.shape[1]), lambda bi, i: (bi, i, 0)) for w in w_segs]
    out_specs += [pl.BlockSpec((1, ng, tm), lambda bi, i: (bi, 0, i))]
    return pl.pallas_call(
        functools.partial(_inproj_kernel, n_seg=n_seg),
        out_shape=out_shape,
        grid=(b, l // tm),
        in_specs=in_specs,
        out_specs=out_specs,
        compiler_params=_params("parallel", "parallel"),
        name="inproj",
    )(x, mods, g.reshape(1, d), *w_segs, w_gt)


def _conv_kernel(p_ref, w_ref, s_ref, o_ref, *, rows, width, transpose):
    v = p_ref[0].astype(F32)
    l, ct = v.shape
    grid = lambda a: a.reshape(rows, width, ct)
    v3, vl3, vr3 = grid(v), grid(pltpu.roll(v, 1, 0)), grid(pltpu.roll(v, l - 1, 0))
    col = lax.broadcasted_iota(jnp.int32, (1, width, ct), 1)
    w = w_ref[...]
    wl = [jnp.where(col == 0, 0.0, w[3 * dr:3 * dr + 1, :][None]) for dr in range(3)]
    wr = [jnp.where(col == width - 1, 0.0, w[3 * dr + 2:3 * dr + 3, :][None]) for dr in range(3)]

    def tap(dr):
        return vl3 * wl[dr] + v3 * w[3 * dr + 1:3 * dr + 2, :][None] + vr3 * wr[dr]

    acc = tap(1)
    if rows > 1:
        edge = jnp.zeros((1, width, ct), F32)
        acc = acc + jnp.concatenate([edge, tap(0)[:-1]], axis=0)
        acc = acc + jnp.concatenate([tap(2)[1:], edge], axis=0)
    acc = acc.reshape(l, ct)
    y = _silu(acc) * s_ref[...]
    o_ref[0] = (y.T if transpose else y).astype(o_ref.dtype)


def _conv(p, w9, scale, rows, col_lo, col_hi, transpose):
    b, l, _ = p.shape
    ct = 512
    j0 = col_lo // ct
    n = col_hi - col_lo
    if transpose:
        out_shape = jax.ShapeDtypeStruct((b, n, l), BF16)
        out_spec = pl.BlockSpec((1, ct, l), lambda bi, j: (bi, j, 0))
    else:
        out_shape = jax.ShapeDtypeStruct((b, l, n), BF16)
        out_spec = pl.BlockSpec((1, l, ct), lambda bi, j: (bi, 0, j))
    return pl.pallas_call(
        functools.partial(_conv_kernel, rows=rows, width=l // rows, transpose=transpose),
        out_shape=out_shape,
        grid=(b, n // ct),
        in_specs=[pl.BlockSpec((1, l, ct), lambda bi, j: (bi, 0, j + j0)),
                  pl.BlockSpec((9, ct), lambda bi, j: (0, j + j0)),
                  pl.BlockSpec((1, ct), lambda bi, j: (0, j + j0))],
        out_specs=out_spec,
        compiler_params=_params("parallel", "parallel"),
        name="conv_kt" if transpose else "conv_q",
    )(p, w9, scale)


def _log_sigmoid(v):
    return jnp.minimum(v, 0.0) - jnp.log1p(jnp.exp(-jnp.abs(v)))


def _chunk_scan(v, op, ident, chunk, reverse):
    l = v.shape[1]
    pos = lax.broadcasted_iota(jnp.int32, v.shape, 1) % chunk
    s = 1
    while s < chunk:
        if reverse:
            shifted = jnp.where(pos < chunk - s, pltpu.roll(v, l - s, 1), ident)
        else:
            shifted = jnp.where(pos >= s, pltpu.roll(v, s, 1), ident)
        v = op(v, shifted)
        s *= 2
    return v


def _gateprep_kernel(g_ref, b_ref, o_ref, *, chunk):
    g = g_ref[0] + b_ref[...]
    i_f = g[0:N_H] * LOG2E
    f_f = _log_sigmoid(g[N_H:2 * N_H]) * LOG2E
    i_b = g[2 * N_H:3 * N_H] * LOG2E
    f_b = _log_sigmoid(g[3 * N_H:4 * N_H]) * LOG2E
    neg = -jnp.inf
    cum_f = _chunk_scan(f_f, jnp.add, 0.0, chunk, False)
    br_f = i_f - cum_f
    pm_f = _chunk_scan(br_f, jnp.maximum, neg, chunk, False)
    cum_b = _chunk_scan(f_b, jnp.add, 0.0, chunk, True)
    br_b = i_b - cum_b
    pm_b = _chunk_scan(br_b, jnp.maximum, neg, chunk, True)
    zero = jnp.zeros_like(cum_f[0:1])
    for h in range(N_H):
        o_ref[0, h] = jnp.concatenate(
            [a[h:h + 1] for a in (cum_f, br_f, pm_f, cum_b, br_b, pm_b)] + [zero, zero], axis=0)


def _gateprep(g_t, b_gates, chunk):
    b, ng, l = g_t.shape
    return pl.pallas_call(
        functools.partial(_gateprep_kernel, chunk=chunk),
        out_shape=jax.ShapeDtypeStruct((b, N_H, N_SCAN_ROWS, l), F32),
        grid=(b,),
        in_specs=[pl.BlockSpec((1, ng, l), lambda bi: (bi, 0, 0)),
                  pl.BlockSpec((ng, 1), lambda bi: (0, 0))],
        out_specs=pl.BlockSpec((1, N_H, N_SCAN_ROWS, l), lambda bi: (bi, 0, 0, 0)),
        compiler_params=_params("parallel"),
        name="gateprep",
    )(g_t, b_gates.reshape(ng, 1))


def _scan_kernel(*refs, chunk, hps, want_h, want_state):
    q_ref, kt_ref, v_ref, rr_ref, c0_ref, m0_ref = refs[:6]
    pos = 6
    if want_h:
        h_ref = refs[pos]
        pos += 1
    if want_state:
        cs_ref, ms_ref = refs[pos:pos + 2]
        pos += 2
    vext_scr = refs[pos]
    if want_h:
        hf_scr, hb_scr = refs[pos + 1:pos + 3]

    l, dh = q_ref.shape[1], q_ref.shape[2] // hps
    nc = l // chunk
    rep = chunk // dh
    jj = lax.broadcasted_iota(jnp.int32, (chunk, chunk), 0)
    ss = lax.broadcasted_iota(jnp.int32, (chunk, chunk), 1)
    causal = (jj >= ss, jj <= ss)

    for hh in range(hps):
        hl = slice(hh * dh, (hh + 1) * dh)
        vext_scr[hh, :, 0:dh] = v_ref[0, :, hl]
        vext_scr[hh, :, dh:2 * dh] = jnp.ones((l, dh), BF16)
        cols = rr_ref[0, hh].T
        cst = [c0_ref[0, d, hh] for d in range(2)]
        mst = [m0_ref[0, d, hh][:, 0:1] for d in range(2)]

        for i in range(nc):
            for d in range(2):
                c = i if d == 0 else nc - 1 - i
                lo = c * chunk
                end = chunk - 1 if d == 0 else 0
                r0 = 3 * d
                qc = q_ref[0, lo:lo + chunk, hl]
                ktc = kt_ref[0, hl, lo:lo + chunk]
                vc = vext_scr[hh, lo:lo + chunk, :]
                cum_c = cols[lo:lo + chunk, r0:r0 + 1]
                pm_c = cols[lo:lo + chunk, r0 + 2:r0 + 3]
                br_r = rr_ref[0, hh, r0 + 1:r0 + 2, lo:lo + chunk]
                tot = rr_ref[0, hh, r0:r0 + 1, lo + end:lo + end + 1]
                pm_end = rr_ref[0, hh, r0 + 2:r0 + 3, lo + end:lo + end + 1]
                m0 = mst[d]
                mm = jnp.broadcast_to(jnp.maximum(m0, pm_c), (chunk, dh))
                d_in = jnp.where(causal[d],
                                 jnp.exp2(br_r - jnp.concatenate([mm] * rep, axis=1)), 0.0)
                d_st = jnp.exp2(m0 - mm)
                s = jnp.dot(qc, ktc, preferred_element_type=F32)
                p = (s * d_in).astype(BF16)
                qd = (qc.astype(F32) * d_st).astype(BF16)
                both = (jnp.dot(p, vc, preferred_element_type=F32)
                        + jnp.dot(qd, cst[d].astype(BF16), preferred_element_type=F32))
                if want_h:
                    m_j = jnp.broadcast_to(cum_c, (chunk, dh)) + mm
                    hc = both[:, 0:dh] / jnp.maximum(jnp.abs(both[:, dh:2 * dh]), jnp.exp2(-m_j))
                    (hf_scr if d == 0 else hb_scr)[hh, lo:lo + chunk, :] = hc
                kw = (ktc.astype(F32) * d_in[end:end + 1, :]).astype(BF16)
                upd = jnp.dot(kw, vc, preferred_element_type=F32)
                dn = d_st[end:end + 1, :]
                cst[d] = jnp.concatenate([dn, dn], axis=1) * cst[d] + upd
                mst[d] = tot + jnp.maximum(m0, pm_end)

        if want_h:
            h_ref[0, :, hl] = (hf_scr[hh] + hb_scr[hh]).astype(h_ref.dtype)
        if want_state:
            for d in range(2):
                cs_ref[0, d, hh] = cst[d]
                ms_ref[0, d, hh] = jnp.broadcast_to(mst[d], (1, LANES))


def _scan(q, kt, v, rr, c0, m0, *, hps, want_h, want_state):
    b, l, dm = v.shape
    dh = dm // N_H
    chunk = min(SCAN_CHUNK, l)
    assert chunk % dh == 0 and l % chunk == 0 and N_H % hps == 0
    tok = pl.BlockSpec((1, l, hps * dh), lambda bi, h: (bi, 0, h))
    in_specs = [tok, pl.BlockSpec((1, hps * dh, l), lambda bi, h: (bi, h, 0)), tok]
    args = [q, kt, v]
    st_c = pl.BlockSpec((1, 2, hps, dh, 2 * dh), lambda bi, h: (bi, 0, h, 0, 0))
    st_m = pl.BlockSpec((1, 2, hps, 1, LANES), lambda bi, h: (bi, 0, h, 0, 0))
    in_specs += [pl.BlockSpec((1, hps, N_SCAN_ROWS, l), lambda bi, h: (bi, h, 0, 0)), st_c, st_m]
    args += [rr, c0, m0]
    out_shape, out_specs = [], []
    scratch = [pltpu.VMEM((hps, l, 2 * dh), BF16)]
    if want_h:
        out_shape.append(jax.ShapeDtypeStruct((b, l, dm), BF16))
        out_specs.append(tok)
        scratch += [pltpu.VMEM((hps, l, dh), F32), pltpu.VMEM((hps, l, dh), F32)]
    if want_state:
        out_shape += [jax.ShapeDtypeStruct(c0.shape, F32), jax.ShapeDtypeStruct(m0.shape, F32)]
        out_specs += [st_c, st_m]
    return pl.pallas_call(
        functools.partial(_scan_kernel, chunk=chunk, hps=hps, want_h=want_h, want_state=want_state),
        out_shape=out_shape,
        grid=(b, N_H // hps),
        in_specs=in_specs,
        out_specs=out_specs,
        scratch_shapes=scratch,
        compiler_params=_params("parallel", "parallel"),
        name="scan_h" if want_h else "scan_state",
    )(*args)


def _twiddle(z, k, n):
    zr, zi = z
    k %= n
    if k == 0:
        return z
    if 4 * k == n:
        return zi, -zr
    if 2 * k == n:
        return -zr, -zi
    if 4 * k == 3 * n:
        return -zi, zr
    wr, wi = math.cos(2.0 * math.pi * k / n), -math.sin(2.0 * math.pi * k / n)
    return zr * wr - zi * wi, zr * wi + zi * wr


def _fft_list(xs):
    n = len(xs)
    if n == 1:
        return xs
    ev, od = _fft_list(xs[0::2]), _fft_list(xs[1::2])
    out = [None] * n
    for k in range(n // 2):
        tr, ti = _twiddle(od[k], k, n)
        out[k] = (ev[k][0] + tr, ev[k][1] + ti)
        out[k + n // 2] = (ev[k][0] - tr, ev[k][1] - ti)
    return out


def _fourier_kernel(u_ref, wc_ref, tw_ref, cs_ref, o_ref, zz_scr, r_scr, *, fg, n1):
    l, df = u_ref.shape[1], u_ref.shape[2]
    n2 = l // n1
    for g in range(df // fg):
        r = jnp.dot(u_ref[0, :, g * fg:(g + 1) * fg], wc_ref[...], preferred_element_type=F32)
        xs = [(r[j * n2:(j + 1) * n2, 0:fg], r[j * n2:(j + 1) * n2, fg:2 * fg]) for j in range(n1)]
        zs = _fft_list(xs)
        for k1 in range(n1):
            zr, zi = zs[k1]
            if k1 > 0:
                tc, ts = tw_ref[0, k1], tw_ref[1, k1]
                zr, zi = zr * tc + zi * ts, zi * tc - zr * ts
            zz_scr[k1, 0:n2, g * fg:(g + 1) * fg] = zr.astype(BF16)
            zz_scr[k1, n2:2 * n2, g * fg:(g + 1) * fg] = zi.astype(BF16)
    pitch = r_scr.shape[1] // n1
    for k1 in range(n1):
        res = jnp.dot(cs_ref[...], zz_scr[k1], preferred_element_type=F32)
        for c in range(df // LANES):
            r_scr[c, k1 * pitch:k1 * pitch + n2, :] = res[:, c * LANES:(c + 1) * LANES]

    def emit(j, carry):
        for c in range(df // LANES):
            a = r_scr[c, pl.ds(2 * j, n1, stride=pitch), :]
            b = r_scr[c, pl.ds(2 * j + 1, n1, stride=pitch), :]
            o_ref[0, pl.ds(pl.multiple_of(2 * n1 * j, 2 * n1), 2 * n1), c * LANES:(c + 1) * LANES] = (
                jnp.concatenate([a, b], axis=0).astype(o_ref.dtype))
        return carry

    lax.fori_loop(0, n2 // 2, emit, 0, unroll=True)


def _dft_angles(rows, cols, n):
    ang = ((rows[:, None] * cols[None, :]) % n).astype(F32) * (2.0 * math.pi / n)
    return jnp.cos(ang), jnp.sin(ang)


def _fourier(u):
    b, l, df = u.shape
    fg = df // N_FG
    n1 = 8
    n2 = l // n1
    ar = lambda n: jnp.arange(n, dtype=jnp.int32)
    cc, sc = _dft_angles(ar(fg), ar(fg), fg)
    wc = (jnp.concatenate([cc, -sc], axis=1) * fg ** -0.5).astype(BF16)
    tc, ts = _dft_angles(ar(n1), ar(n2), l)
    tw = jnp.broadcast_to(jnp.stack([tc, ts])[..., None], (2, n1, n2, fg))
    c2, s2 = _dft_angles(ar(n2), ar(n2), n2)
    cs = (jnp.concatenate([c2, s2], axis=1) * l ** -0.5).astype(BF16)
    const = lambda a: pl.BlockSpec(a.shape, lambda bi: (0,) * a.ndim, pipeline_mode=pl.Buffered(1))
    pitch = n2 + 4
    return pl.pallas_call(
        functools.partial(_fourier_kernel, fg=fg, n1=n1),
        out_shape=jax.ShapeDtypeStruct((b, l, df), BF16),
        grid=(b,),
        in_specs=[pl.BlockSpec((1, l, df), lambda bi: (bi, 0, 0)), const(wc), const(tw), const(cs)],
        out_specs=pl.BlockSpec((1, l, df), lambda bi: (bi, 0, 0)),
        scratch_shapes=[pltpu.VMEM((n1, 2 * n2, df), BF16),
                        pltpu.VMEM((df // LANES, n1 * pitch, LANES), F32)],
        compiler_params=_params("parallel"),
        name="fourier",
    )(u, wc, tw, cs)


def _merge_kernel(fu_ref, hm_ref, x_ref, mod_ref, g1_ref, wbr_ref, wog_ref, wf_ref, wm_ref, wo_ref,
                  g2_ref, wr1_ref, wr2_ref, x1_ref, hx_ref, aff_ref):
    tm, d = x_ref.shape[1], x_ref.shape[2]
    ne = aff_ref.shape[1]
    for r0 in range(0, tm, MERGE_SUBTILE):
        rows = slice(r0, r0 + MERGE_SUBTILE)
        x = x_ref[0, rows, :]
        h1 = _rms_mod(x, g1_ref[...], mod_ref[0, 0:1, :], mod_ref[0, 1:2, :]).astype(BF16)
        br = jnp.dot(h1, wbr_ref[...], preferred_element_type=F32)
        og = jnp.dot(h1, wog_ref[...], preferred_element_type=F32)
        yf = jnp.dot(fu_ref[0, rows, :], wf_ref[...], preferred_element_type=F32)
        hm = (jax.nn.sigmoid(og) * hm_ref[0, rows, :].astype(F32)).astype(BF16)
        ym = jnp.dot(hm, wm_ref[...], preferred_element_type=F32)
        merged = (jax.nn.sigmoid(br[:, 0:d]) * yf + jax.nn.sigmoid(br[:, d:2 * d]) * ym).astype(BF16)
        o = jnp.dot(merged, wo_ref[...], preferred_element_type=F32)
        x1 = x + mod_ref[0, 2:3, :] * o
        x1_ref[0, rows, :] = x1
        h2 = _rms_mod(x1, g2_ref[...], mod_ref[0, 3:4, :], mod_ref[0, 4:5, :])
        hi = h2.astype(BF16)
        hx_ref[0, rows, :] = hi
        lo = (h2 - hi.astype(F32)).astype(BF16)
        r = (jnp.dot(hi, wr1_ref[...], preferred_element_type=F32)
             + jnp.dot(lo, wr2_ref[...], preferred_element_type=F32))
        rt = r.T
        logits = rt[0:ne, :] + rt[ne:2 * ne, :]
        e = jnp.exp(logits - jnp.max(logits, axis=0, keepdims=True))
        aff_ref[0, :, rows] = e / jnp.sum(e, axis=0, keepdims=True)


def _merge(fu, hm, x, mods, g1, w_br, w_og, wf, wm, wo, g2, w_router, tm):
    b, l, d = x.shape
    tm = min(tm, l)
    ne = w_router.shape[1]
    whi = w_router.astype(BF16)
    wlo = (w_router - whi.astype(F32)).astype(BF16)
    pad = jnp.zeros((d, LANES - 2 * ne), BF16)
    wr1 = jnp.concatenate([whi, wlo, pad], axis=1)
    wr2 = jnp.concatenate([whi, jnp.zeros((d, ne), BF16), pad], axis=1)
    tok = lambda n: pl.BlockSpec((1, tm, n), lambda bi, i: (bi, i, 0))
    const = lambda a: pl.BlockSpec(a.shape, lambda bi, i: (0, 0), pipeline_mode=pl.Buffered(1))
    vec = pl.BlockSpec((1, d), lambda bi, i: (0, 0))
    return pl.pallas_call(
        _merge_kernel,
        out_shape=[jax.ShapeDtypeStruct((b, l, d), F32),
                   jax.ShapeDtypeStruct((b, l, d), BF16),
                   jax.ShapeDtypeStruct((b, ne, l), F32)],
        grid=(b, l // tm),
        in_specs=[tok(fu.shape[2]), tok(hm.shape[2]), tok(d),
                  pl.BlockSpec((1, N_MOD, d), lambda bi, i: (bi, 0, 0)),
                  vec, const(w_br), const(w_og), const(wf), const(wm), const(wo), vec,
                  const(wr1), const(wr2)],
        out_specs=[tok(d), tok(d), pl.BlockSpec((1, ne, tm), lambda bi, i: (bi, 0, i))],
        compiler_params=_params("parallel", "parallel"),
        name="merge",
    )(fu, hm, x, mods, g1.reshape(1, d), w_br, w_og, wf, wm, wo, g2.reshape(1, d), wr1, wr2)


def _topk_kernel(a_ref, u_ref, slot_ref, before_ref, *, k):
    a = a_ref[...]
    l = a.shape[1]
    total = lambda m: jnp.sum(m, axis=1, keepdims=True)
    ge = lambda bits: jnp.where(a >= pltpu.bitcast(bits, F32), 1.0, 0.0)

    def body(i, t):
        cand = t | lax.shift_left(jnp.int32(1), 30 - i)
        return jnp.where(total(ge(cand)) >= k, cand, t)

    t = lax.fori_loop(0, 31, body, jnp.zeros((a.shape[0], 1), jnp.int32))
    chosen = ge(t + 1)
    rem = ge(t) - chosen
    need = k - total(chosen)
    idx = lax.broadcasted_iota(jnp.int32, a.shape, 1).astype(F32)
    for _ in range(N_TIE_EXTRACT):
        cur = jnp.where(rem > 0.0, a, -1.0)
        is_mx = jnp.where(cur == jnp.max(cur, axis=1, keepdims=True), rem, 0.0)
        first = jnp.min(jnp.where(is_mx > 0.0, idx, float(l)), axis=1, keepdims=True)
        pick = jnp.where(idx == first, jnp.where(need > 0.0, 1.0, 0.0), 0.0)
        chosen = chosen + pick
        rem = rem - pick
        need = need - total(pick)
    before = jnp.dot(rem.astype(BF16), u_ref[...], preferred_element_type=F32)
    sel = chosen + jnp.where(before < need, rem, 0.0)
    rank = jnp.dot(sel.astype(BF16), u_ref[...], preferred_element_type=F32)
    slot_ref[...] = jnp.where(sel > 0.0, rank, -1.0)
    before_ref[...] = rank


def _topk(aff_t, k):
    r, l = aff_t.shape
    tr = min(128, r)
    idx = jnp.arange(l, dtype=jnp.int32)
    upper = (idx[:, None] < idx[None, :]).astype(BF16)
    return pl.pallas_call(
        functools.partial(_topk_kernel, k=k),
        out_shape=[jax.ShapeDtypeStruct((r, l), F32)] * 2,
        grid=(r // tr,),
        in_specs=[pl.BlockSpec((tr, l), lambda i: (i, 0)),
                  pl.BlockSpec((l, l), lambda i: (0, 0), pipeline_mode=pl.Buffered(1))],
        out_specs=[pl.BlockSpec((tr, l), lambda i: (i, 0))] * 2,
        compiler_params=_params("parallel"),
        name="topk",
    )(aff_t, upper)


def _moe_kernel(cnt_ref, hx_hbm, srow_ref, arow_ref, slot_ref, wg_ref, wu_ref, wd_ref, x1_ref,
                mod_ref, g_ref, xp_ref, o_ref, y_scr, xs_scr, hx_ref, hx_sem, *, cap, ne, nt):
    bi = pl.program_id(0)
    s = pl.program_id(1)
    ng = hx_ref.shape[0]
    win = MOE_WINDOW
    per_sample = ne * (nt + 1)

    def hx_copy(group):
        return pltpu.make_async_copy(hx_hbm.at[pl.ds(group * ng, ng)], hx_ref, hx_sem)

    @pl.when(jnp.logical_and(s == 0, bi == 0))
    def _():
        hx_copy(0).start()

    @pl.when(s == 0)
    def _():
        hx_copy(bi).wait()
        for gi in range(ng):
            y_scr[gi, ne * cap:ne * cap + win, :] = jnp.zeros((win, y_scr.shape[2]), BF16)

    @pl.when(jnp.logical_and(s == ne, bi + 1 < pl.num_programs(0)))
    def _():
        hx_copy(bi + 1).start()

    @pl.when(s < ne)
    def _():
        l = hx_ref.shape[1]
        gw = MOE_GATHER_WINDOW
        tpb = MOE_GATHER_TILES if nt % MOE_GATHER_TILES == 0 else 1
        tb, nb = tpb * MOE_TILE, nt // tpb
        xs_scr[...] = jnp.zeros(xs_scr.shape, F32)
        row_i = lax.broadcasted_iota(jnp.int32, (gw, tb), 0)
        lows, n_pass = [], 0
        for gi in range(ng):
            base = (bi * ng + gi) * per_sample + s * (nt + 1)
            for j in range(nb):
                low = (cnt_ref[base + j * tpb] // ROW_ALIGN) * ROW_ALIGN
                lows.append(low)
                n_pass = jnp.maximum(n_pass, (cnt_ref[base + (j + 1) * tpb] - low + gw - 1) // gw)

        def gather_pass(w, carry):
            for gi in range(ng):
                for j in range(nb):
                    lo = pl.multiple_of(jnp.minimum(lows[gi * nb + j] + w * gw, cap), ROW_ALIGN)
                    srow_blk = srow_ref[gi, 0, :, j * tb:(j + 1) * tb]
                    p = jnp.where(srow_blk == (row_i + lo).astype(F32), 1.0, 0.0).astype(BF16)
                    xs_scr[gi, pl.ds(lo, gw), :] += jnp.dot(
                        p, hx_ref[gi, j * tb:(j + 1) * tb, :], preferred_element_type=F32)
            return carry

        gather_pass(0, 0)
        lax.fori_loop(1, n_pass, gather_pass, 0)
        slot_i = lax.broadcasted_iota(jnp.int32, (cap, l), 0).astype(F32)
        for gi in range(ng):
            xs = xs_scr[gi, 0:cap, :].astype(BF16)
            gate = jnp.dot(xs, wg_ref[0], preferred_element_type=F32)
            up = jnp.dot(xs, wu_ref[0], preferred_element_type=F32)
            hid = (_silu(gate) * up).astype(BF16)
            y = jnp.dot(hid, wd_ref[0], preferred_element_type=F32)
            hit = srow_ref[gi, 0] == slot_i
            vals = jnp.sum(jnp.where(hit, arow_ref[gi, 0], 0.0), axis=1, keepdims=True)
            y_scr[gi, pl.ds(pl.multiple_of(s * cap, cap), cap), :] = (y * vals).astype(BF16)

    @pl.when(s >= ne)
    def _():
        tt, d = MOE_TILE, x1_ref.shape[2]
        tps = x1_ref.shape[1] // tt
        gi = (s - ne) // (nt // tps)
        j0 = ((s - ne) % (nt // tps)) * tps
        lane = lax.broadcasted_iota(jnp.int32, (1, ne * win), 1)
        grp, off = lane // win, lane % win
        slots = slot_ref[0].T.astype(BF16)
        tiles, n_pass = [], 0
        for u in range(tps):
            base = (bi * ng + gi) * per_sample + j0 + u
            lows = []
            for e in range(ne):
                start = cnt_ref[base + e * (nt + 1)]
                end = cnt_ref[base + e * (nt + 1) + 1]
                low = (start // SLOT_ALIGN) * SLOT_ALIGN
                lows.append(low)
                n_pass = jnp.maximum(n_pass, (end - low + win - 1) // win)
            sexp = jnp.dot(slots[u * tt:(u + 1) * tt, :], xp_ref[...], preferred_element_type=F32)
            tiles.append((lows, sexp))

        def one_pass(w, accs):
            out = []
            for (lows, sexp), acc in zip(tiles, accs):
                tgt, rows = off, []
                for e in range(ne):
                    lo = jnp.minimum(lows[e] + w * win, cap)
                    tgt = jnp.where(grp == e, off + lo, tgt)
                    rows.append(
                        y_scr[gi, pl.ds(pl.multiple_of(e * cap + lo, SLOT_ALIGN), win), :])
                pt = jnp.where(sexp == tgt.astype(F32), 1.0, 0.0).astype(BF16)
                out.append(acc + jnp.dot(pt, jnp.concatenate(rows, axis=0),
                                         preferred_element_type=F32))
            return tuple(out)

        def finish(extra_passes):
            accs = one_pass(0, tuple(jnp.zeros((tt, d), F32) for _ in range(tps)))
            if extra_passes:
                accs = lax.fori_loop(1, n_pass, one_pass, accs)
            for u, moe in enumerate(accs):
                v = x1_ref[0, u * tt:(u + 1) * tt, :] + mod_ref[gi, 5:6, :] * moe
                ms = jnp.mean(v * v, axis=-1, keepdims=True)
                o_ref[0, u * tt:(u + 1) * tt, :] = v * lax.rsqrt(ms + EPS) * g_ref[...]

        pl.when(n_pass <= 1)(lambda: finish(False))
        pl.when(n_pass > 1)(lambda: finish(True))


def _moe(hx, slot_t, before_t, aff_t, wg, wu, wd, x1, mods, g_final, cap):
    b, l, d = hx.shape
    ne, _, f = wg.shape
    tt = MOE_TILE
    nt = l // tt
    win = MOE_WINDOW
    ng = MOE_GROUP if b % MOE_GROUP == 0 else 1
    assert cap % SLOT_ALIGN == 0 and win % SLOT_ALIGN == 0 and (win & (win - 1)) == 0
    slot_row = slot_t.reshape(b, ne, 1, l)
    aff_row = aff_t.reshape(b, ne, 1, l)
    cnt = jnp.concatenate([before_t[:, :, ::tt], jnp.full((b, ne, 1), cap, F32)], axis=2)
    cnt = cnt.astype(jnp.int32).reshape(b * ne * (nt + 1))
    lane = jnp.arange(ne * win, dtype=jnp.int32)
    expand = (lane[None, :] // win == jnp.arange(ne, dtype=jnp.int32)[:, None]).astype(BF16)
    ex = lambda s: jnp.minimum(s, ne - 1)
    tps = MOE_TILES_PER_STEP if nt % MOE_TILES_PER_STEP == 0 else 1
    ts, spt = tps * tt, nt // tps
    sample = lambda bi, s: bi * ng + jnp.maximum(s - ne, 0) // spt
    tile = lambda s: jnp.maximum(s - ne, 0) % spt
    row = pl.BlockSpec((ng, 1, 1, l), lambda bi, s, c: (bi, ex(s), 0, 0))
    wspec = lambda k, n: pl.BlockSpec((1, k, n), lambda bi, s, c: (ex(s), 0, 0))
    grid_spec = pltpu.PrefetchScalarGridSpec(
        num_scalar_prefetch=1,
        grid=(b // ng, ne + ng * spt),
        in_specs=[pl.BlockSpec(memory_space=pl.ANY),
                  row, row,
                  pl.BlockSpec((1, ne, ts), lambda bi, s, c: (sample(bi, s), 0, tile(s))),
                  wspec(d, f), wspec(d, f), wspec(f, d),
                  pl.BlockSpec((1, ts, d), lambda bi, s, c: (sample(bi, s), tile(s), 0)),
                  pl.BlockSpec((ng, N_MOD, d), lambda bi, s, c: (bi, 0, 0)),
                  pl.BlockSpec((1, d), lambda bi, s, c: (0, 0)),
                  pl.BlockSpec((ne, ne * win), lambda bi, s, c: (0, 0))],
        out_specs=pl.BlockSpec((1, ts, d), lambda bi, s, c: (sample(bi, s), tile(s), 0)),
        scratch_shapes=[pltpu.VMEM((ng, ne * cap + win, d), BF16),
                        pltpu.VMEM((ng, cap + MOE_GATHER_WINDOW, d), F32),
                        pltpu.VMEM((ng, l, d), BF16),
                        pltpu.SemaphoreType.DMA(())])
    return pl.pallas_call(
        functools.partial(_moe_kernel, cap=cap, ne=ne, nt=nt),
        out_shape=jax.ShapeDtypeStruct((b, l, d), F32),
        grid_spec=grid_spec,
        compiler_params=_params("arbitrary", "arbitrary"),
        name="moe",
    )(cnt, hx, slot_row, aff_row, slot_t, wg, wu, wd, x1, mods, g_final.reshape(1, d), expand)


def kernel(x, c, ctx, c_ctx, w_ada, b_ada, g_norm1, w_in, b_gates, w_conv, w_fourier, w_mlstm,
           w_out, g_norm2, w_router, w_gate_e, w_up_e, w_down_e, g_final):
    assert w_ada.shape[0] == 1, "single-layer stack"
    b, l, d = x.shape
    lc = ctx.shape[1]
    d_f = d // 2
    d_m = d // 2
    dh = d_m // N_H
    ne = w_router.shape[2]
    cap = CAP_FACTOR * l // ne
    off_qk = d_f
    off_v = off_qk + 2 * d_m
    off_o = off_v + d_m
    off_g = off_o + d_m
    off_br = off_g + N_GATE_KINDS * N_H

    n_c = b + 1
    n_pad = -n_c % 8
    c_all = jnp.concatenate([c, c_ctx[None, :], jnp.zeros((n_pad, d), F32)], axis=0)
    mods = _ada(c_all, w_ada[0], b_ada[0]).reshape(n_c + n_pad, N_MOD, d)

    w = w_in[0]
    wb = w.astype(BF16)
    w_f, w_qk, w_v, w_o, w_br = (wb[:, 0:off_qk], wb[:, off_qk:off_v], wb[:, off_v:off_o],
                                 wb[:, off_o:off_g], wb[:, off_br:])
    w_gt = wb[:, off_g:off_br].T
    w9 = w_conv[0].reshape(9, 2 * d_m)
    qk_scale = jnp.concatenate([jnp.ones((1, d_m), F32), jnp.full((1, d_m), dh ** -0.5, F32)], axis=1)

    pc_qk, pc_v, gc_t = _inproj(ctx, mods, lambda bi, i: (b, 0, 0), g_norm1[0],
                                [w_qk, w_v], w_gt, tm=lc)
    q_c = _conv(pc_qk, w9, qk_scale, 1, 0, d_m, False)
    kt_c = _conv(pc_qk, w9, qk_scale, 1, d_m, 2 * d_m, True)
    rr_c = _gateprep(gc_t, b_gates[0], min(SCAN_CHUNK, lc))
    c0 = jnp.zeros((b, 2, N_H, dh, 2 * dh), F32)
    m0 = jnp.zeros((b, 2, N_H, 1, LANES), F32)
    c_seed, m_seed = _scan(q_c, kt_c, pc_v, rr_c, c0, m0, hps=N_H, want_h=False,
                           want_state=True)

    p_f, p_qk, p_v, g_t = _inproj(x, mods, lambda bi, i: (bi, 0, 0), g_norm1[0],
                                  [w_f, w_qk, w_v], w_gt, tm=1024)
    q_x = _conv(p_qk, w9, qk_scale, l // GRID_W, 0, d_m, False)
    kt_x = _conv(p_qk, w9, qk_scale, l // GRID_W, d_m, 2 * d_m, True)
    rr_x = _gateprep(g_t, b_gates[0], SCAN_CHUNK)
    (hm,) = _scan(q_x, kt_x, p_v, rr_x, c_seed, m_seed, hps=SCAN_HEADS_PER_STEP, want_h=True,
                  want_state=False)
    fu = _fourier(p_f)
    x1, hx2, aff_t = _merge(fu, hm, x, mods, g_norm1[0], w_br, w_o, w_fourier[0].astype(BF16),
                            w_mlstm[0].astype(BF16), w_out[0].astype(BF16), g_norm2[0], w_router[0],
                            tm=1024)

    slot_t, before_t = _topk(aff_t.reshape(b * ne, l), cap)
    return _moe(hx2, slot_t.reshape(b, ne, l), before_t.reshape(b, ne, l), aff_t,
                w_gate_e[0].astype(BF16), w_up_e[0].astype(BF16), w_down_e[0].astype(BF16),
                x1, mods, g_final, cap)
```
